```python
import math
import jax, jax.numpy as jnp
from jax import lax
import numpy as np

D_MODEL = 1024
BATCH = 8
SEQ = 2048
DEPTH = 4
DEC_BATCH = 128
DEC_SEQ = 4
PAST_LEN = 16384
PAGE_SIZE = 128

N_MIXERS = 2
N_HGRN = (DEPTH + N_MIXERS - 1) // N_MIXERS
N_GDN = DEPTH // N_MIXERS
HGRN_HEADS = 8
HGRN_DK = 128
HGRN_DV = D_MODEL // HGRN_HEADS
HGRN_F = HGRN_HEADS * HGRN_DK
HGRN_IN = 2 * HGRN_F + 2 * D_MODEL
GDN_HEADS = 8
GDN_DK = 128
GDN_DV = 128
GDN_QK = GDN_HEADS * GDN_DK
GDN_V = GDN_HEADS * GDN_DV
GDN_CONV = 4
GDN_CONV_DIM = 2 * GDN_QK + GDN_V
GDN_IN = GDN_CONV_DIM + GDN_V + 2 * GDN_HEADS
D_FF = 4 * D_MODEL
CHUNK = 64
EPS = 1e-6
LB_FLOOR = 1e-30

kernel_name = "hgrn2_gated_deltanet_hybrid_step"

F32 = jnp.float32


def _rmsnorm(x, w):
    xf = x.astype(F32)
    y = xf * lax.rsqrt(jnp.mean(xf * xf, axis=-1, keepdims=True) + EPS)
    return (y * w.astype(F32)).astype(x.dtype)


def _l2norm(x):
    return x * lax.rsqrt(jnp.sum(x * x, axis=-1, keepdims=True) + EPS)


def _chunking(T):
    C = min(CHUNK, T)
    n = -(-T // C)
    return C, n, n * C - T


def _pad_time(x, pad):
    return jnp.pad(x, [(0, 0), (0, pad)] + [(0, 0)] * (x.ndim - 2))


def _to_chunks(x, C, n):
    x = x.reshape((x.shape[0], n, C) + x.shape[2:])
    return jnp.moveaxis(x, (1, 3), (0, 2))


def _from_chunks(o, T):
    o = jnp.moveaxis(o, (0, 2), (1, 3))
    n, C = o.shape[1], o.shape[2]
    return o.reshape((o.shape[0], n * C) + o.shape[3:])[:, :T]


def _hgrn_recurrence(q, k, v, log_f, S0):
    T = q.shape[1]
    C, n, pad = _chunking(T)
    q, k, v, log_f = [_to_chunks(_pad_time(a, pad), C, n) for a in (q, k, v, log_f)]
    b = jnp.cumsum(log_f, axis=3)
    causal = jnp.tril(jnp.ones((C, C), bool))[:, :, None]

    def step(S, inp):
        qc, kc, vc, bc = inp
        diff = bc[:, :, :, None, :] - bc[:, :, None, :, :]
        dec = jnp.where(causal, jnp.exp(jnp.where(causal, diff, 0.0)), 0.0)
        A = jnp.einsum('bhtk,bhsk,bhtsk->bhts', qc, kc, dec)
        bl = bc[:, :, -1]
        o = (jnp.einsum('bhtk,bhkv->bhtv', qc * jnp.exp(bc), S)
             + jnp.einsum('bhts,bhsv->bhtv', A, vc))
        S = (jnp.exp(bl)[..., None] * S
             + jnp.einsum('bhsk,bhsv->bhkv', kc * jnp.exp(bl[:, :, None, :] - bc), vc))
        return S, o

    S, o = lax.scan(step, S0, (q, k, v, b))
    return _from_chunks(o, T), S


def _gdn_recurrence(q, k, v, beta, g, S0):
    T = q.shape[1]
    C, n, pad = _chunking(T)
    q, k, v, beta, g = [_to_chunks(_pad_time(a, pad), C, n) for a in (q, k, v, beta, g)]
    G = jnp.cumsum(g, axis=-1)
    incl = jnp.tril(jnp.ones((C, C), bool))
    strict = jnp.tril(jnp.ones((C, C), bool), k=-1)
    diff = G[..., :, None] - G[..., None, :]
    decay = jnp.where(incl, jnp.exp(jnp.where(incl, diff, 0.0)), 0.0)
    kk = jnp.einsum('nbhtk,nbhsk->nbhts', k, k)
    M = jnp.eye(C, dtype=F32) + jnp.where(strict, beta[..., :, None] * kk * decay, 0.0)
    u = lax.linalg.triangular_solve(M, v * beta[..., None], left_side=True, lower=True)
    w = lax.linalg.triangular_solve(M, k * (beta * jnp.exp(G))[..., None],
                                    left_side=True, lower=True)
    Aqk = jnp.einsum('nbhtk,nbhsk->nbhts', q, k) * decay
    qg = q * jnp.exp(G)[..., None]
    kd = k * jnp.exp(G[..., -1:] - G)[..., None]
    gl = jnp.exp(G[..., -1])

    def step(S, inp):
        uc, wc, qc, ac, kc, glc = inp
        v_new = uc - jnp.einsum('bhtk,bhkv->bhtv', wc, S)
        o = jnp.einsum('bhtk,bhkv->bhtv', qc, S) + jnp.einsum('bhts,bhsv->bhtv', ac, v_new)
        S = glc[..., None, None] * S + jnp.einsum('bhsk,bhsv->bhkv', kc, v_new)
        return S, o

    S, o = lax.scan(step, S0, (u, w, qg, Aqk, kd, gl))
    return _from_chunks(o, T), S


def _causal_conv(x, buf, w):
    T = x.shape[1]
    xp = jnp.concatenate([buf.astype(x.dtype), x], axis=1)
    y = xp[:, 0:T] * w[0]
    for j in range(1, GDN_CONV):
        y = y + xp[:, j:j + T] * w[j]
    return jax.nn.silu(y), xp[:, T:]


def _hgrn_mixer(h, w_in, lb, out_norm, w_out, S0):
    B, T, _ = h.shape
    proj = h @ w_in
    q, fz, i, g = jnp.split(proj, [HGRN_F, 2 * HGRN_F, 2 * HGRN_F + D_MODEL], axis=-1)
    q = jax.nn.silu(q.astype(F32)).reshape(B, T, HGRN_HEADS, HGRN_DK)
    fz = fz.astype(F32).reshape(B, T, HGRN_HEADS, HGRN_DK)
    lb = lb.astype(F32).reshape(HGRN_HEADS, HGRN_DK)
    log_f = jnp.logaddexp(jnp.log(jnp.maximum(lb, LB_FLOOR)),
                          jnp.log1p(-lb) + jax.nn.log_sigmoid(fz))
    k = (1.0 - lb) * jax.nn.sigmoid(-fz)
    v = i.astype(F32).reshape(B, T, HGRN_HEADS, HGRN_DV)
    o, S = _hgrn_recurrence(q, k, v, log_f, S0.astype(F32))
    o = _rmsnorm(o, out_norm) * jax.nn.silu(g.astype(F32).reshape(B, T, HGRN_HEADS, HGRN_DV))
    return o.reshape(B, T, D_MODEL).astype(h.dtype) @ w_out, S


def _gdn_mixer(h, w_in, conv_w, A_log, dt_bias, out_norm, w_out, S0, buf):
    B, T, _ = h.shape
    proj = h @ w_in
    qkv, z, a, b = jnp.split(
        proj, [GDN_CONV_DIM, GDN_CONV_DIM + GDN_V, GDN_CONV_DIM + GDN_V + GDN_HEADS], axis=-1)
    qkv, new_buf = _causal_conv(qkv, buf, conv_w)
    q, k, v = jnp.split(qkv.astype(F32), [GDN_QK, 2 * GDN_QK], axis=-1)
    q = _l2norm(q.reshape(B, T, GDN_HEADS, GDN_DK)) * (GDN_DK ** -0.5)
    k = _l2norm(k.reshape(B, T, GDN_HEADS, GDN_DK))
    v = v.reshape(B, T, GDN_HEADS, GDN_DV)
    beta = jax.nn.sigmoid(b.astype(F32))
    g = -jnp.exp(A_log.astype(F32)) * jax.nn.softplus(a.astype(F32) + dt_bias.astype(F32))
    o, S = _gdn_recurrence(q, k, v, beta, g, S0.astype(F32))
    o = _rmsnorm(o, out_norm) * jax.nn.silu(z.astype(F32).reshape(B, T, GDN_HEADS, GDN_DV))
    return o.reshape(B, T, GDN_V).astype(h.dtype) @ w_out, S, new_buf


def _trunk(x, s_hgrn, s_gdn, s_conv, norm_mix, norm_mlp, norm_final, hgrn_w_in, hgrn_lb_eff,
           hgrn_out_norm, hgrn_w_out, gdn_w_in, gdn_conv_w, gdn_A_log, gdn_dt_bias,
           gdn_out_norm, gdn_w_out, mlp_w_up, mlp_w_down):
    new_h, new_g, new_c = [], [], []
    for i in range(DEPTH):
        j = i // N_MIXERS
        h = _rmsnorm(x, norm_mix[i])
        if i % N_MIXERS == 0:
            y, S = _hgrn_mixer(h, hgrn_w_in[j], hgrn_lb_eff[j], hgrn_out_norm[j],
                               hgrn_w_out[j], s_hgrn[j])
            new_h.append(S)
        else:
            y, S, c = _gdn_mixer(h, gdn_w_in[j], gdn_conv_w[j], gdn_A_log[j], gdn_dt_bias[j],
                                 gdn_out_norm[j], gdn_w_out[j], s_gdn[j], s_conv[j])
            new_g.append(S)
            new_c.append(c)
        x = x + y
        u = _rmsnorm(x, norm_mlp[i]) @ mlp_w_up[i]
        x = x + jnp.square(jax.nn.relu(u)) @ mlp_w_down[i]
    return _rmsnorm(x, norm_final), jnp.stack(new_h), jnp.stack(new_g), jnp.stack(new_c)


def setup_inputs(seed: int = 0) -> dict:
    key = jax.random.key(seed)
    ks = jax.random.split(key, 20)

    def nrm(k, shape, scale):
        return jax.random.normal(k, shape, F32) * scale

    dt = jnp.exp(jax.random.uniform(ks[15], (N_GDN, GDN_HEADS), F32,
                                    math.log(1e-3), math.log(1e-1)))
    return {
        "x_prompt": nrm(ks[0], (BATCH, SEQ, D_MODEL), 1.0),
        "x_sample": nrm(ks[1], (DEC_BATCH, DEC_SEQ, D_MODEL), 1.0),
        "state_hgrn": nrm(ks[2], (N_HGRN, DEC_BATCH, HGRN_HEADS, HGRN_DK, HGRN_DV), 0.5),
        "state_gdn": nrm(ks[3], (N_GDN, DEC_BATCH, GDN_HEADS, GDN_DK, GDN_DV), 0.1),
        "state_gdn_conv": nrm(ks[4], (N_GDN, DEC_BATCH, GDN_CONV - 1, GDN_CONV_DIM), 1.0),
        "norm_mix": 1.0 + nrm(ks[5], (DEPTH, D_MODEL), 0.02),
        "norm_mlp": 1.0 + nrm(ks[6], (DEPTH, D_MODEL), 0.02),
        "norm_final": 1.0 + nrm(ks[7], (D_MODEL,), 0.02),
        "hgrn_w_in": nrm(ks[8], (N_HGRN, D_MODEL, HGRN_IN), D_MODEL ** -0.5),
        "hgrn_lb": nrm(ks[9], (N_HGRN, HGRN_F), 0.5),
        "hgrn_out_norm": 1.0 + nrm(ks[10], (N_HGRN, HGRN_DV), 0.02),
        "hgrn_w_out": nrm(ks[11], (N_HGRN, D_MODEL, D_MODEL), D_MODEL ** -0.5),
        "gdn_w_in": nrm(ks[12], (N_GDN, D_MODEL, GDN_IN), D_MODEL ** -0.5),
        "gdn_conv_w": nrm(ks[13], (N_GDN, GDN_CONV, GDN_CONV_DIM), GDN_CONV ** -0.5),
        "gdn_A_log": jnp.log(jax.random.uniform(ks[14], (N_GDN, GDN_HEADS), F32, 1.0, 16.0)),
        "gdn_dt_bias": dt + jnp.log(-jnp.expm1(-dt)),
        "gdn_out_norm": 1.0 + nrm(ks[16], (N_GDN, GDN_DV), 0.02),
        "gdn_w_out": nrm(ks[17], (N_GDN, GDN_V, D_MODEL), GDN_V ** -0.5),
        "mlp_w_up": nrm(ks[18], (DEPTH, D_MODEL, D_FF), D_MODEL ** -0.5),
        "mlp_w_down": nrm(ks[19], (DEPTH, D_FF, D_MODEL), D_FF ** -0.5),
    }


def reference(x_prompt, x_sample, state_hgrn, state_gdn, state_gdn_conv, norm_mix, norm_mlp,
              norm_final, hgrn_w_in, hgrn_lb, hgrn_out_norm, hgrn_w_out, gdn_w_in, gdn_conv_w,
              gdn_A_log, gdn_dt_bias, gdn_out_norm, gdn_w_out, mlp_w_up, mlp_w_down):
    lb = jax.nn.softmax(hgrn_lb.astype(F32), axis=0)
    lb_eff = jnp.cumsum(lb, axis=0) - lb[0]
    weights = (norm_mix, norm_mlp, norm_final, hgrn_w_in, lb_eff, hgrn_out_norm, hgrn_w_out,
               gdn_w_in, gdn_conv_w, gdn_A_log, gdn_dt_bias, gdn_out_norm, gdn_w_out,
               mlp_w_up, mlp_w_down)
    B = x_prompt.shape[0]
    zh = jnp.zeros((N_HGRN, B, HGRN_HEADS, HGRN_DK, HGRN_DV), F32)
    zg = jnp.zeros((N_GDN, B, GDN_HEADS, GDN_DK, GDN_DV), F32)
    zc = jnp.zeros((N_GDN, B, GDN_CONV - 1, GDN_CONV_DIM), x_prompt.dtype)
    y_prompt, hgrn_prompt, gdn_prompt, conv_prompt = _trunk(x_prompt, zh, zg, zc, *weights)
    y_sample, hgrn_sample, gdn_sample, conv_sample = _trunk(
        x_sample, state_hgrn, state_gdn, state_gdn_conv, *weights)
    return (y_prompt, y_sample, hgrn_prompt, hgrn_sample, gdn_prompt, gdn_sample,
            conv_prompt, conv_sample)
```

```python
import functools

import jax
import jax.numpy as jnp
import numpy as np
from jax import lax
from jax.experimental import pallas as pl
from jax.experimental.pallas import tpu as pltpu

F32 = jnp.float32
BF16 = jnp.bfloat16

D_MODEL = 1024
N_HEADS = 8
HEAD_DIM = 128
D_FF = 4 * D_MODEL
FF_CHUNK = 1024
GDN_CONV = 4
GDN_CONV_DIM = 3 * N_HEADS * HEAD_DIM
GDN_IN_PAD = GDN_CONV_DIM + N_HEADS * HEAD_DIM + HEAD_DIM
GB_BLOCK = (GDN_CONV_DIM + N_HEADS * HEAD_DIM) // HEAD_DIM
EPS = 1e-6
LB_FLOOR = 1e-30
LANES = 128
SUBLANES = 8
ROW_TILE = 512
PROMPT_CHUNK = 64
PROMPT_BLOCK = 256
SAMPLE_T_PAD = 8
VMEM_LIMIT = 56 * 1024 * 1024


def _dot(a, b):
    return jnp.dot(a, b, preferred_element_type=F32)


def _dot_f32(a, b):
    return jnp.dot(a, b, preferred_element_type=F32, precision=lax.Precision.HIGHEST)


def _dot_nt(a, b, precision=None):
    return lax.dot_general(a, b, (((1,), (1,)), ((), ())), preferred_element_type=F32,
                           precision=precision)


def _dot_tn(a, b):
    return lax.dot_general(a, b, (((0,), (0,)), ((), ())), preferred_element_type=F32)


def _sigmoid(x):
    return 1.0 / (1.0 + jnp.exp(-x))


def _softplus(x):
    return jnp.maximum(x, 0.0) + jnp.log1p(jnp.exp(-jnp.abs(x)))


def _rms(x, w):
    return x * lax.rsqrt(jnp.mean(x * x, axis=-1, keepdims=True) + EPS) * w


def _rms_matmul_kernel(x_ref, nw_ref, w_ref, o_ref):
    h = _rms(x_ref[...], nw_ref[...]).astype(BF16)
    o_ref[...] = _dot(h, w_ref[...])


def _rms_matmul(x, nw, w):
    m, d = x.shape
    n = w.shape[1]
    tm = min(ROW_TILE, m)
    return pl.pallas_call(
        _rms_matmul_kernel,
        grid=(m // tm,),
        in_specs=[
            pl.BlockSpec((tm, d), lambda i: (i, 0)),
            pl.BlockSpec((1, d), lambda i: (0, 0)),
            pl.BlockSpec((d, n), lambda i: (0, 0), pipeline_mode=pl.Buffered(1)),
        ],
        out_specs=pl.BlockSpec((tm, n), lambda i: (i, 0)),
        out_shape=jax.ShapeDtypeStruct((m, n), F32),
        compiler_params=pltpu.CompilerParams(
            dimension_semantics=("arbitrary",), vmem_limit_bytes=VMEM_LIMIT),
        name="rms_matmul",
    )(x, nw.reshape(1, d), w)


def _out_mlp_kernel(o_ref, x_ref, wo_ref, nw_ref, wu_ref, wd_ref, nf_ref, xo_ref, y_ref=None):
    x1 = x_ref[...] + _dot(o_ref[...].astype(BF16), wo_ref[...])
    h = _rms(x1, nw_ref[...]).astype(BF16)
    acc = x1
    for c in range(D_FF // FF_CHUNK):
        u = _dot(h, wu_ref[:, c * FF_CHUNK:(c + 1) * FF_CHUNK])
        a = jnp.square(jnp.maximum(u, 0.0)).astype(BF16)
        acc = acc + _dot(a, wd_ref[c * FF_CHUNK:(c + 1) * FF_CHUNK, :])
    xo_ref[...] = acc
    if y_ref is not None:
        y_ref[...] = _rms(acc, nf_ref[...])


def _out_mlp(o, x, w_out, nw, w_up, w_down, nf, final):
    m, d = x.shape
    tm = min(ROW_TILE, m)
    row = pl.BlockSpec((tm, d), lambda i: (i, 0))
    vec = pl.BlockSpec((1, d), lambda i: (0, 0))

    def whole(shape):
        return pl.BlockSpec(shape, lambda i: (0, 0), pipeline_mode=pl.Buffered(1))

    out_shape = [jax.ShapeDtypeStruct((m, d), F32)]
    out_specs = [row]
    if final:
        out_shape.append(jax.ShapeDtypeStruct((m, d), F32))
        out_specs.append(row)
    return pl.pallas_call(
        _out_mlp_kernel,
        grid=(m // tm,),
        in_specs=[row, row, whole(w_out.shape), vec, whole(w_up.shape), whole(w_down.shape), vec],
        out_specs=out_specs,
        out_shape=out_shape,
        compiler_params=pltpu.CompilerParams(
            dimension_semantics=("arbitrary",), vmem_limit_bytes=VMEM_LIMIT),
        name="out_mlp",
    )(o, x, w_out, nw.reshape(1, d), w_up, w_down, nf.reshape(1, d))


def _hgrn_tables(c):
    levels = int(np.log2(c))
    assert 2 ** levels == c
    t = np.arange(c)[:, None]
    r = np.arange(c)[None, :]
    summing = [(r <= t), (r > t)]
    mask = [(r == t)]
    for l in range(levels):
        h = 2 ** l
        start = (t // (2 * h)) * (2 * h)
        upper = (t - start) >= h
        ref = start + h - 1
        summing.append(np.where(upper, (r > ref) & (r <= t), (r > t) & (r <= ref)))
        mask.append(upper & (r >= start) & (r <= ref))
    return (np.concatenate(summing, 0).astype(np.float32),
            np.concatenate(mask, 0).astype(np.float32), levels)


def _hgrn_kernel(*refs, layer, chunk, n_sub, levels, t_valid, has_state):
    if has_state:
        (q_ref, f_ref, v_ref, g_ref, lb_ref, on_ref, sum_ref, msk_ref, s0_ref,
         o_ref, sout_ref, st_ref) = refs
    else:
        (q_ref, f_ref, v_ref, g_ref, lb_ref, on_ref, sum_ref, msk_ref,
         o_ref, sout_ref, st_ref) = refs
    step = pl.program_id(2)
    c = chunk

    @pl.when(step == 0)
    def _():
        if has_state:
            st_ref[...] = s0_ref[...].T
        else:
            st_ref[...] = jnp.zeros_like(st_ref)

    lbraw = lb_ref[...]
    e = jnp.exp(lbraw - jnp.max(lbraw, axis=0, keepdims=True))
    sm = e / jnp.sum(e, axis=0, keepdims=True)
    lb = sm[0:1, :]
    for i in range(1, layer + 1):
        lb = lb + sm[i:i + 1, :]
    lb = lb - sm[0:1, :]
    log_lb = jnp.log(jnp.maximum(lb, LB_FLOOR))
    log_1m = jnp.log1p(-lb)

    row = lax.broadcasted_iota(jnp.int32, (c, HEAD_DIM), 0)
    onw = on_ref[...]

    def body(j, carry):
        rows = pl.ds(pl.multiple_of(j * c, c), c)
        qr = q_ref[rows, :]
        fz = f_ref[rows, :]
        v = v_ref[rows, :]
        gr = g_ref[rows, :]
        q = qr * _sigmoid(qr)
        log_sig = jnp.minimum(fz, 0.0) - jnp.log1p(jnp.exp(-jnp.abs(fz)))
        lo = log_1m + log_sig
        lf = jnp.maximum(log_lb, lo) + jnp.log1p(jnp.exp(-jnp.abs(log_lb - lo)))
        k = (1.0 - lb) * _sigmoid(-fz)
        if t_valid < c:
            live = row < t_valid
            q = jnp.where(live, q, 0.0)
            k = jnp.where(live, k, 0.0)
            v = jnp.where(live, v, 0.0)
            lf = jnp.where(live, lf, 0.0)

        sums = _dot_f32(sum_ref[...], lf)
        b = sums[0:c]
        b_rem = sums[c:2 * c]
        st = st_ref[...]
        vb = v.astype(BF16)
        o = _dot_nt((q * jnp.exp(b)).astype(BF16), st.astype(BF16))
        a = msk_ref[0:c, :] * _dot_nt(q.astype(BF16), k.astype(BF16))
        for l in range(levels):
            upper = ((row >> l) & 1) == 1
            x = (jnp.where(upper, q, k) * jnp.exp(sums[(2 + l) * c:(3 + l) * c])).astype(BF16)
            a = a + msk_ref[(1 + l) * c:(2 + l) * c, :] * _dot_nt(x, x)
        o = o + _dot(a.astype(BF16), vb)
        kh = (k * jnp.exp(b_rem)).astype(BF16)
        st_ref[...] = st * jnp.exp(b[c - 1:c, :]) + _dot_tn(vb, kh)

        o_ref[rows, :] = _rms(o, onw) * (gr * _sigmoid(gr))
        return carry

    lax.fori_loop(0, n_sub, body, 0)

    @pl.when(step == pl.num_programs(2) - 1)
    def _():
        sout_ref[...] = st_ref[...].T


def _hgrn_scan(proj, lb_raw, out_norm, s0, layer, chunk, block, t_valid):
    bsz, t, _ = proj.shape
    n_lb = lb_raw.shape[0]
    summing, mask, levels = _hgrn_tables(chunk)
    has_state = s0 is not None

    def col(k):
        return pl.BlockSpec((None, block, HEAD_DIM), lambda b, h, s: (b, s, k * N_HEADS + h))

    def const(shape):
        return pl.BlockSpec(shape, lambda b, h, s: (0, 0))

    state_spec = pl.BlockSpec((None, None, HEAD_DIM, HEAD_DIM), lambda b, h, s: (b, h, 0, 0))
    in_specs = [col(0), col(1), col(2), col(3),
                pl.BlockSpec((n_lb, HEAD_DIM), lambda b, h, s: (0, h)),
                const((1, HEAD_DIM)), const(summing.shape), const(mask.shape)]
    args = [proj, proj, proj, proj, lb_raw, out_norm.reshape(1, HEAD_DIM),
            jnp.asarray(summing), jnp.asarray(mask)]
    if has_state:
        in_specs.append(state_spec)
        args.append(s0)
    kern = functools.partial(_hgrn_kernel, layer=layer, chunk=chunk, n_sub=block // chunk,
                             levels=levels, t_valid=t_valid, has_state=has_state)
    return pl.pallas_call(
        kern,
        grid=(bsz, N_HEADS, t // block),
        in_specs=in_specs,
        out_specs=[pl.BlockSpec((None, block, HEAD_DIM), lambda b, h, s: (b, s, h)), state_spec],
        out_shape=[jax.ShapeDtypeStruct((bsz, t, D_MODEL), F32),
                   jax.ShapeDtypeStruct((bsz, N_HEADS, HEAD_DIM, HEAD_DIM), F32)],
        scratch_shapes=[pltpu.VMEM((HEAD_DIM, HEAD_DIM), F32)],
        compiler_params=pltpu.CompilerParams(
            dimension_semantics=("arbitrary", "arbitrary", "arbitrary"),
            vmem_limit_bytes=VMEM_LIMIT),
        name="hgrn_scan",
    )(*args)


def _gdn_kernel(*refs, chunk, n_sub, block, t_valid, has_state):
    if has_state:
        (q_ref, k_ref, v_ref, z_ref, gb_ref, cwq_ref, cwk_ref, cwv_ref, al_ref, dt_ref, on_ref,
         s0_ref, bq_ref, bk_ref, bv_ref, o_ref, sout_ref, s_ref, xq, xk, xv, yq, yk, yv) = refs
    else:
        (q_ref, k_ref, v_ref, z_ref, gb_ref, cwq_ref, cwk_ref, cwv_ref, al_ref, dt_ref, on_ref,
         o_ref, sout_ref, s_ref, xq, xk, xv, yq, yk, yv) = refs
    head = pl.program_id(1)
    step = pl.program_id(2)
    c = chunk
    tail = SUBLANES

    @pl.when(step == 0)
    def _():
        if has_state:
            s_ref[...] = s0_ref[...]
            for xs, b_ref in ((xq, bq_ref), (xk, bk_ref), (xv, bv_ref)):
                xs[tail - (GDN_CONV - 1):tail, :] = b_ref[...]
        else:
            s_ref[...] = jnp.zeros_like(s_ref)
            for xs in (xq, xk, xv):
                xs[0:tail, :] = jnp.zeros((tail, HEAD_DIM), F32)

    for xs, src, cw, ys in ((xq, q_ref, cwq_ref, yq), (xk, k_ref, cwk_ref, yk),
                            (xv, v_ref, cwv_ref, yv)):
        xs[tail:tail + block, :] = src[...]
        y = xs[pl.ds(tail - (GDN_CONV - 1), block), :] * cw[0:1, :]
        for j in range(1, GDN_CONV):
            y = y + xs[pl.ds(tail - (GDN_CONV - 1) + j, block), :] * cw[j:j + 1, :]
        ys[...] = y * _sigmoid(y)
        xs[0:tail, :] = xs[block:block + tail, :]

    lane = lax.broadcasted_iota(jnp.int32, (c, LANES), 1)
    row = lax.broadcasted_iota(jnp.int32, (c, LANES), 0)
    tt = lax.broadcasted_iota(jnp.int32, (c, c), 0)
    ss = lax.broadcasted_iota(jnp.int32, (c, c), 1)
    incl = ss <= tt
    strict = ss < tt
    tri = incl.astype(F32)
    ones = jnp.ones((c, LANES), F32)
    onw = on_ref[...]
    neg_a = -jnp.exp(al_ref[...])
    dtb = dt_ref[...]

    def body(j, carry):
        rows = pl.ds(pl.multiple_of(j * c, c), c)
        qc = yq[rows, :]
        kc = yk[rows, :]
        v = yv[rows, :]
        z = z_ref[rows, :]
        gb = gb_ref[rows, :]
        g_all = neg_a * _softplus(gb + dtb)
        beta_all = _sigmoid(gb)
        g_m = jnp.where(lane == head, g_all, 0.0)
        beta_m = jnp.where(lane == head + N_HEADS, beta_all, 0.0)
        qn = qc * lax.rsqrt(jnp.sum(qc * qc, axis=-1, keepdims=True) + EPS) * (HEAD_DIM ** -0.5)
        kn = kc * lax.rsqrt(jnp.sum(kc * kc, axis=-1, keepdims=True) + EPS)
        if t_valid < c:
            live = row < t_valid
            g_m = jnp.where(live, g_m, 0.0)
            beta_m = jnp.where(live, beta_m, 0.0)
            qn = jnp.where(live, qn, 0.0)
            kn = jnp.where(live, kn, 0.0)
            v = jnp.where(live, v, 0.0)
        beta = jnp.sum(beta_m, axis=-1, keepdims=True)

        gcum_m = _dot_f32(tri, g_m)
        gcol = jnp.sum(gcum_m, axis=-1, keepdims=True)
        grow = _dot_nt(ones, gcum_m, precision=lax.Precision.HIGHEST)
        glast = jnp.sum(gcum_m[c - 1:c, :], axis=-1, keepdims=True)
        decay = jnp.where(incl, jnp.exp(jnp.where(incl, gcol - grow, 0.0)), 0.0)

        kb = kn.astype(BF16)
        kk = _dot_nt(kb, kb)
        low = jnp.where(strict, beta * kk * decay, 0.0)
        n = -low
        p = low
        width = 1
        while 2 * width < c:
            p = _dot_f32(p, p)
            width *= 2
            n = n + p + _dot_f32(n, p)

        eg = jnp.exp(gcol)
        rhs = jnp.concatenate([v * beta, kn * (beta * eg)], axis=1)
        uw = rhs + _dot(n.astype(BF16), rhs.astype(BF16))
        u = uw[:, 0:HEAD_DIM]
        w = uw[:, HEAD_DIM:2 * HEAD_DIM]
        aqk = _dot_nt(qn.astype(BF16), kb) * decay
        qg = qn * eg
        kd = kn * jnp.exp(glast - gcol)

        s = s_ref[...]
        sb = s.astype(BF16)
        v_new = u - _dot(w.astype(BF16), sb)
        vnb = v_new.astype(BF16)
        o = _dot(qg.astype(BF16), sb) + _dot(aqk.astype(BF16), vnb)
        s_ref[...] = jnp.exp(glast) * s + _dot_tn(kd.astype(BF16), vnb)

        o_ref[rows, :] = _rms(o, onw) * (z * _sigmoid(z))
        return carry

    lax.fori_loop(0, n_sub, body, 0)

    @pl.when(step == pl.num_programs(2) - 1)
    def _():
        sout_ref[...] = s_ref[...]


def _gdn_scan(proj, conv_w, a_log, dt_bias, out_norm, s0, buf, chunk, block, t_valid):
    bsz, t, _ = proj.shape
    has_state = s0 is not None

    def col(k):
        return pl.BlockSpec((None, block, HEAD_DIM), lambda b, h, s: (b, s, k * N_HEADS + h))

    def cw(k):
        return pl.BlockSpec((GDN_CONV, HEAD_DIM), lambda b, h, s: (0, k * N_HEADS + h))

    def bufspec(k):
        return pl.BlockSpec((None, GDN_CONV - 1, HEAD_DIM), lambda b, h, s: (b, 0, k * N_HEADS + h))

    def const(shape):
        return pl.BlockSpec(shape, lambda b, h, s: (0, 0))

    def head_row(x):
        return jnp.zeros((1, LANES), F32).at[0, 0:N_HEADS].set(x)

    state_spec = pl.BlockSpec((None, None, HEAD_DIM, HEAD_DIM), lambda b, h, s: (b, h, 0, 0))
    in_specs = [col(0), col(1), col(2), col(3),
                pl.BlockSpec((None, block, LANES), lambda b, h, s: (b, s, GB_BLOCK)),
                cw(0), cw(1), cw(2), const((1, LANES)), const((1, LANES)), const((1, HEAD_DIM))]
    args = [proj, proj, proj, proj, proj, conv_w, conv_w, conv_w,
            head_row(a_log), head_row(dt_bias), out_norm.reshape(1, HEAD_DIM)]
    if has_state:
        in_specs += [state_spec, bufspec(0), bufspec(1), bufspec(2)]
        args += [s0, buf, buf, buf]
    kern = functools.partial(_gdn_kernel, chunk=chunk, n_sub=block // chunk, block=block,
                             t_valid=t_valid, has_state=has_state)
    xs = pltpu.VMEM((block + 2 * SUBLANES, HEAD_DIM), F32)
    ys = pltpu.VMEM((block, HEAD_DIM), F32)
    return pl.pallas_call(
        kern,
        grid=(bsz, N_HEADS, t // block),
        in_specs=in_specs,
        out_specs=[pl.BlockSpec((None, block, HEAD_DIM), lambda b, h, s: (b, s, h)), state_spec],
        out_shape=[jax.ShapeDtypeStruct((bsz, t, D_MODEL), F32),
                   jax.ShapeDtypeStruct((bsz, N_HEADS, HEAD_DIM, HEAD_DIM), F32)],
        scratch_shapes=[pltpu.VMEM((HEAD_DIM, HEAD_DIM), F32), xs, xs, xs, ys, ys, ys],
        compiler_params=pltpu.CompilerParams(
            dimension_semantics=("arbitrary", "arbitrary", "arbitrary"),
            vmem_limit_bytes=VMEM_LIMIT),
        name="gdn_scan",
    )(*args)


def kernel(x_prompt, x_sample, state_hgrn, state_gdn, state_gdn_conv, norm_mix, norm_mlp, norm_final,
           hgrn_w_in, hgrn_lb, hgrn_out_norm, hgrn_w_out, gdn_w_in, gdn_conv_w, gdn_A_log,
           gdn_dt_bias, gdn_out_norm, gdn_w_out, mlp_w_up, mlp_w_down):
    depth = norm_mix.shape[0]
    pb, pt, d = x_prompt.shape
    sb, st, _ = x_sample.shape
    assert pt % PROMPT_BLOCK == 0 and st <= SAMPLE_T_PAD and st >= GDN_CONV - 1

    groups = [
        dict(b=pb, t=pt, tv=PROMPT_CHUNK, chunk=PROMPT_CHUNK, block=PROMPT_BLOCK, x=x_prompt),
        dict(b=sb, t=SAMPLE_T_PAD, tv=st, chunk=SAMPLE_T_PAD, block=SAMPLE_T_PAD,
             x=jnp.pad(x_sample, ((0, 0), (0, SAMPLE_T_PAD - st), (0, 0)))),
    ]
    states = [dict(h=None, g=None, c=None),
              dict(h=state_hgrn, g=state_gdn, c=state_gdn_conv)]
    xs = [g["x"].reshape(g["b"] * g["t"], d) for g in groups]
    new_h, new_g, new_c = ([], []), ([], []), ([], [])
    ys = [None, None]

    for i in range(depth):
        j = i // 2
        last = i == depth - 1
        w_out = (hgrn_w_out if i % 2 == 0 else gdn_w_out)[j].astype(BF16)
        w_up = mlp_w_up[i].astype(BF16)
        w_down = mlp_w_down[i].astype(BF16)
        if i % 2 == 0:
            w_in = hgrn_w_in[j].astype(BF16)
        else:
            w_in = jnp.pad(gdn_w_in[j], ((0, 0), (0, GDN_IN_PAD - gdn_w_in.shape[2]))).astype(BF16)
        for gi, (g, s) in enumerate(zip(groups, states)):
            proj = _rms_matmul(xs[gi], norm_mix[i], w_in).reshape(g["b"], g["t"], -1)
            if i % 2 == 0:
                o, s_new = _hgrn_scan(proj, hgrn_lb, hgrn_out_norm[j],
                                      None if s["h"] is None else s["h"][j],
                                      j, g["chunk"], g["block"], g["tv"])
                new_h[gi].append(s_new)
            else:
                o, s_new = _gdn_scan(proj, gdn_conv_w[j], gdn_A_log[j], gdn_dt_bias[j],
                                     gdn_out_norm[j],
                                     None if s["g"] is None else s["g"][j],
                                     None if s["c"] is None else s["c"][j],
                                     g["chunk"], g["block"], g["tv"])
                new_g[gi].append(s_new)
                tv = g["tv"] if gi == 1 else g["t"]
                new_c[gi].append(proj[:, tv - (GDN_CONV - 1):tv, 0:GDN_CONV_DIM])
            res = _out_mlp(o.reshape(-1, d), xs[gi], w_out, norm_mlp[i], w_up, w_down,
                           norm_final, last)
            xs[gi] = res[0]
            if last:
                ys[gi] = res[1]

    y_prompt = ys[0].reshape(pb, pt, d)
    y_sample = ys[1].reshape(sb, SAMPLE_T_PAD, d)[:, 0:st]
    return (y_prompt, y_sample, jnp.stack(new_h[0]), jnp.stack(new_h[1]),
            jnp.stack(new_g[0]), jnp.stack(new_g[1]), jnp.stack(new_c[0]), jnp.stack(new_c[1]))
```

```python
import functools

import jax
import jax.numpy as jnp
import numpy as np
from jax import lax
from jax.experimental import pallas as pl
from jax.experimental.pallas import tpu as pltpu

F32 = jnp.float32
BF16 = jnp.bfloat16

D_MODEL = 1024
N_HEADS = 8
HEAD_DIM = 128
D_FF = 4 * D_MODEL
FF_CHUNK = 1024
GDN_CONV = 4
GDN_CONV_DIM = 3 * D_MODEL
GDN_Z_COL = GDN_CONV_DIM
GDN_GB_COL = GDN_CONV_DIM + D_MODEL
GDN_IN_PAD = GDN_GB_COL + HEAD_DIM
EPS = 1e-6
LB_FLOOR = 1e-30
LANES = 128
SUBLANES = 8
ROW_TILE = 512
PROMPT_CHUNK = 64
PROMPT_BLOCK = 256
SAMPLE_T_PAD = 8
VMEM_LIMIT = 56 * 1024 * 1024


def _dot(a, b):
    return jnp.dot(a, b, preferred_element_type=F32)


def _dot_nt(a, b):
    return lax.dot_general(a, b, (((1,), (1,)), ((), ())), preferred_element_type=F32)


def _dot_tn(a, b):
    return lax.dot_general(a, b, (((0,), (0,)), ((), ())), preferred_element_type=F32)


def _dot_table(table, x):
    hi = x.astype(BF16)
    r = x - hi.astype(F32)
    mid = r.astype(BF16)
    lo = (r - mid.astype(F32)).astype(BF16)
    return _dot(table, hi) + _dot(table, mid) + _dot(table, lo)


def _sigmoid(x):
    return 1.0 / (1.0 + jnp.exp(-x))


def _softplus(x):
    return jnp.maximum(x, 0.0) + jnp.log1p(jnp.exp(-jnp.abs(x)))


def _rms(x, w):
    return x * lax.rsqrt(jnp.mean(x * x, axis=-1, keepdims=True) + EPS) * w


def _level_masks(c):
    tt = lax.broadcasted_iota(jnp.int32, (c, c), 0)
    ss = lax.broadcasted_iota(jnp.int32, (c, c), 1)
    levels = int(np.log2(c))
    assert 2 ** levels == c
    return [(((tt >> l) & 1) == 1) & (((ss >> l) & 1) == 0) & ((tt >> (l + 1)) == (ss >> (l + 1)))
            for l in range(levels)], tt, ss


def _rms_matmul_kernel(x_ref, nw_ref, w_ref, o_ref):
    h = _rms(x_ref[...], nw_ref[...]).astype(BF16)
    o_ref[...] = _dot(h, w_ref[...])


def _rms_matmul(x, nw, w):
    m, d = x.shape
    n = w.shape[1]
    tm = min(ROW_TILE, m)
    return pl.pallas_call(
        _rms_matmul_kernel,
        grid=(m // tm,),
        in_specs=[
            pl.BlockSpec((tm, d), lambda i: (i, 0)),
            pl.BlockSpec((1, d), lambda i: (0, 0)),
            pl.BlockSpec((d, n), lambda i: (0, 0), pipeline_mode=pl.Buffered(1)),
        ],
        out_specs=pl.BlockSpec((tm, n), lambda i: (i, 0)),
        out_shape=jax.ShapeDtypeStruct((m, n), F32),
        compiler_params=pltpu.CompilerParams(
            dimension_semantics=("arbitrary",), vmem_limit_bytes=VMEM_LIMIT),
        name="rms_matmul",
    )(x, nw.reshape(1, d), w)


def _out_mlp_kernel(o_ref, x_ref, wo_ref, nw_ref, wu_ref, wd_ref, nf_ref, xo_ref, y_ref=None):
    x1 = x_ref[...] + _dot(o_ref[...].astype(BF16), wo_ref[...])
    h = _rms(x1, nw_ref[...]).astype(BF16)
    acc = x1
    for c in range(D_FF // FF_CHUNK):
        u = _dot(h, wu_ref[:, c * FF_CHUNK:(c + 1) * FF_CHUNK])
        a = jnp.square(jnp.maximum(u, 0.0)).astype(BF16)
        acc = acc + _dot(a, wd_ref[c * FF_CHUNK:(c + 1) * FF_CHUNK, :])
    xo_ref[...] = acc
    if y_ref is not None:
        y_ref[...] = _rms(acc, nf_ref[...])


def _out_mlp(o, x, w_out, nw, w_up, w_down, nf, final):
    m, d = x.shape
    tm = min(ROW_TILE, m)
    row = pl.BlockSpec((tm, d), lambda i: (i, 0))
    vec = pl.BlockSpec((1, d), lambda i: (0, 0))

    def whole(shape):
        return pl.BlockSpec(shape, lambda i: (0, 0), pipeline_mode=pl.Buffered(1))

    out_shape = [jax.ShapeDtypeStruct((m, d), F32)]
    out_specs = [row]
    if final:
        out_shape.append(jax.ShapeDtypeStruct((m, d), F32))
        out_specs.append(row)
    return pl.pallas_call(
        _out_mlp_kernel,
        grid=(m // tm,),
        in_specs=[row, row, whole(w_out.shape), vec, whole(w_up.shape), whole(w_down.shape), vec],
        out_specs=out_specs,
        out_shape=out_shape,
        compiler_params=pltpu.CompilerParams(
            dimension_semantics=("arbitrary",), vmem_limit_bytes=VMEM_LIMIT),
        name="out_mlp",
    )(o, x, w_out, nw.reshape(1, d), w_up, w_down, nf.reshape(1, d))


def _hgrn_tables(c):
    levels = int(np.log2(c))
    t = np.arange(c)[:, None]
    r = np.arange(c)[None, :]
    summing = [(r <= t), (r > t)]
    for l in range(levels):
        h = 2 ** l
        start = (t // (2 * h)) * (2 * h)
        upper = (t - start) >= h
        ref = start + h - 1
        summing.append(np.where(upper, (r > ref) & (r <= t), (r > t) & (r <= ref)))
    return np.concatenate(summing, 0).astype(np.float32)


def _hgrn_kernel(*refs, layer, chunk, n_sub, t_valid, has_state):
    if has_state:
        p_ref, lb_ref, on_ref, sum_ref, s0_ref, o_ref, sout_ref, st_ref = refs
    else:
        p_ref, lb_ref, on_ref, sum_ref, o_ref, sout_ref, st_ref = refs
    step = pl.program_id(1)
    c = chunk
    d = D_MODEL

    @pl.when(step == 0)
    def _():
        for h in range(N_HEADS):
            if has_state:
                st_ref[h] = s0_ref[h].T
            else:
                st_ref[h] = jnp.zeros((HEAD_DIM, HEAD_DIM), F32)

    lbraw = lb_ref[...]
    e = jnp.exp(lbraw - jnp.max(lbraw, axis=0, keepdims=True))
    sm = e / jnp.sum(e, axis=0, keepdims=True)
    lb = sm[0:1, :]
    for i in range(1, layer + 1):
        lb = lb + sm[i:i + 1, :]
    lb = lb - sm[0:1, :]
    lb_floor = jnp.maximum(lb, LB_FLOOR)
    one_m_lb = 1.0 - lb

    masks, tt, ss = _level_masks(c)
    diag = tt == ss
    row = lax.broadcasted_iota(jnp.int32, (c, HEAD_DIM), 0)
    uppers = [((row >> l) & 1) == 1 for l in range(len(masks))]
    onw = on_ref[...]
    table = sum_ref[...]

    def body(j, carry):
        rows = pl.ds(pl.multiple_of(j * c, c), c)
        qr = p_ref[rows, 0:d]
        fz = p_ref[rows, d:2 * d]
        v_all = p_ref[rows, 2 * d:3 * d]
        gr = p_ref[rows, 3 * d:4 * d]
        q_all = qr * _sigmoid(qr)
        gate_all = gr * _sigmoid(gr)
        ez = jnp.exp(-jnp.abs(fz))
        rz = 1.0 / (1.0 + ez)
        pos = fz >= 0.0
        lf_all = jnp.log(lb_floor + one_m_lb * jnp.where(pos, rz, ez * rz))
        k_all = one_m_lb * jnp.where(pos, ez * rz, rz)
        if t_valid < c:
            live = lax.broadcasted_iota(jnp.int32, (c, d), 0) < t_valid
            q_all = jnp.where(live, q_all, 0.0)
            k_all = jnp.where(live, k_all, 0.0)
            v_all = jnp.where(live, v_all, 0.0)
            lf_all = jnp.where(live, lf_all, 0.0)
        ex_all = jnp.exp(_dot_table(table, lf_all))

        heads = range(N_HEADS)
        cs = [slice(h * HEAD_DIM, (h + 1) * HEAD_DIM) for h in heads]
        q = [q_all[:, s] for s in cs]
        k = [k_all[:, s] for s in cs]
        vb = [v_all[:, s].astype(BF16) for s in cs]
        st = [st_ref[h] for h in heads]
        o = [_dot_nt((q[h] * ex_all[0:c, cs[h]]).astype(BF16), st[h].astype(BF16)) for h in heads]
        a = [jnp.where(diag, _dot_nt(q[h].astype(BF16), k[h].astype(BF16)), 0.0) for h in heads]
        for l, (m, upper) in enumerate(zip(masks, uppers)):
            x = [(jnp.where(upper, q[h], k[h]) * ex_all[(2 + l) * c:(3 + l) * c, cs[h]]).astype(BF16)
                 for h in heads]
            a = [a[h] + jnp.where(m, _dot_nt(x[h], x[h]), 0.0) for h in heads]
        o = [o[h] + _dot(a[h].astype(BF16), vb[h]) for h in heads]
        kh = [(k[h] * ex_all[c:2 * c, cs[h]]).astype(BF16) for h in heads]
        st_new = [st[h] * ex_all[c - 1:c, cs[h]] + _dot_tn(vb[h], kh[h]) for h in heads]
        for h in heads:
            st_ref[h] = st_new[h]
        o_all = jnp.concatenate([_rms(o[h], onw) for h in heads], axis=1)
        o_ref[rows, :] = o_all * gate_all
        return carry

    lax.fori_loop(0, n_sub, body, 0)

    @pl.when(step == pl.num_programs(1) - 1)
    def _():
        for h in range(N_HEADS):
            sout_ref[h] = st_ref[h].T


def _hgrn_scan(proj, lb_raw, out_norm, s0, layer, chunk, block, t_valid):
    bsz, t, width = proj.shape
    table = jnp.asarray(_hgrn_tables(chunk), dtype=BF16)
    has_state = s0 is not None

    def const(shape):
        return pl.BlockSpec(shape, lambda b, s: (0, 0))

    state_spec = pl.BlockSpec((None, N_HEADS, HEAD_DIM, HEAD_DIM), lambda b, s: (b, 0, 0, 0))
    in_specs = [pl.BlockSpec((None, block, width), lambda b, s: (b, s, 0)),
                const(lb_raw.shape), const((1, HEAD_DIM)), const(table.shape)]
    args = [proj, lb_raw, out_norm.reshape(1, HEAD_DIM), table]
    if has_state:
        in_specs.append(state_spec)
        args.append(s0)
    kern = functools.partial(_hgrn_kernel, layer=layer, chunk=chunk, n_sub=block // chunk,
                             t_valid=t_valid, has_state=has_state)
    return pl.pallas_call(
        kern,
        grid=(bsz, t // block),
        in_specs=in_specs,
        out_specs=[pl.BlockSpec((None, block, D_MODEL), lambda b, s: (b, s, 0)), state_spec],
        out_shape=[jax.ShapeDtypeStruct((bsz, t, D_MODEL), F32),
                   jax.ShapeDtypeStruct((bsz, N_HEADS, HEAD_DIM, HEAD_DIM), F32)],
        scratch_shapes=[pltpu.VMEM((N_HEADS, HEAD_DIM, HEAD_DIM), F32)],
        compiler_params=pltpu.CompilerParams(
            dimension_semantics=("arbitrary", "arbitrary"), vmem_limit_bytes=VMEM_LIMIT),
        name="hgrn_scan",
    )(*args)


def _gdn_kernel(*refs, chunk, n_sub, block, t_valid, has_state):
    if has_state:
        (p_ref, cw_ref, al_ref, dt_ref, on_ref, tri_ref, s0_ref, buf_ref,
         o_ref, sout_ref, s_ref, xs, ys) = refs
    else:
        (p_ref, cw_ref, al_ref, dt_ref, on_ref, tri_ref,
         o_ref, sout_ref, s_ref, xs, ys) = refs
    step = pl.program_id(1)
    c = chunk
    d = D_MODEL
    tail = SUBLANES
    taps = GDN_CONV - 1

    @pl.when(step == 0)
    def _():
        if has_state:
            s_ref[...] = s0_ref[...]
            xs[tail - taps:tail, :] = buf_ref[...]
        else:
            s_ref[...] = jnp.zeros_like(s_ref)
            xs[0:tail, :] = jnp.zeros((tail, GDN_CONV_DIM), F32)

    xs[tail:tail + block, :] = p_ref[:, 0:GDN_CONV_DIM]
    y = xs[pl.ds(tail - taps, block), :] * cw_ref[0:1, :]
    for j in range(1, GDN_CONV):
        y = y + xs[pl.ds(tail - taps + j, block), :] * cw_ref[j:j + 1, :]
    ys[...] = y * _sigmoid(y)
    xs[0:tail, :] = xs[block:block + tail, :]

    masks, tt, ss = _level_masks(c)
    incl = ss <= tt
    strict = ss < tt
    onw = on_ref[...]
    neg_a = -jnp.exp(al_ref[...])
    dtb = dt_ref[...]
    tri = tri_ref[...]

    def body(j, carry):
        rows = pl.ds(pl.multiple_of(j * c, c), c)
        gb = p_ref[rows, GDN_GB_COL:GDN_GB_COL + LANES]
        g_all = neg_a * _softplus(gb + dtb)
        beta_all = _sigmoid(gb)
        if t_valid < c:
            live = lax.broadcasted_iota(jnp.int32, (c, LANES), 0) < t_valid
            g_all = jnp.where(live, g_all, 0.0)
            beta_all = jnp.where(live, beta_all, 0.0)
        gcum = _dot_table(tri, g_all)
        gcum_t = jnp.concatenate([gcum, jnp.zeros((LANES - c, LANES), F32)], axis=0).T

        heads = range(N_HEADS)
        qkv = ys[rows, :]
        z_all = p_ref[rows, GDN_Z_COL:GDN_Z_COL + d]
        qn, kn, v, gcol, beta, decay = [], [], [], [], [], []
        for h in heads:
            qc = qkv[:, h * HEAD_DIM:(h + 1) * HEAD_DIM]
            kc = qkv[:, d + h * HEAD_DIM:d + (h + 1) * HEAD_DIM]
            vh = qkv[:, 2 * d + h * HEAD_DIM:2 * d + (h + 1) * HEAD_DIM]
            qh = qc * lax.rsqrt(jnp.sum(qc * qc, axis=-1, keepdims=True) + EPS) * (HEAD_DIM ** -0.5)
            kh = kc * lax.rsqrt(jnp.sum(kc * kc, axis=-1, keepdims=True) + EPS)
            if t_valid < c:
                qh = jnp.where(live, qh, 0.0)
                kh = jnp.where(live, kh, 0.0)
                vh = jnp.where(live, vh, 0.0)
            qn.append(qh)
            kn.append(kh)
            v.append(vh)
            gc = jnp.broadcast_to(gcum[:, h:h + 1], (c, LANES))
            grow = jnp.broadcast_to(gcum_t[h:h + 1, 0:c], (c, c))
            gcol.append(gc)
            beta.append(jnp.broadcast_to(beta_all[:, N_HEADS + h:N_HEADS + h + 1], (c, LANES)))
            decay.append(jnp.where(incl, jnp.exp(jnp.where(incl, gc[:, 0:c] - grow, 0.0)), 0.0))

        kb = [kn[h].astype(BF16) for h in heads]
        both = [_dot_nt(jnp.concatenate([kb[h], qn[h].astype(BF16)], axis=0), kb[h])
                for h in heads]
        low = [jnp.where(strict, beta[h][:, 0:c] * both[h][0:c] * decay[h], 0.0) for h in heads]
        n = [-jnp.where(masks[0], low[h], 0.0) for h in heads]
        for m in masks[1:]:
            cl = [jnp.where(m, low[h], 0.0) for h in heads]
            nb = [n[h].astype(BF16) for h in heads]
            t1 = [cl[h] + _dot(nb[h], cl[h].astype(BF16)) for h in heads]
            n = [n[h] - (t1[h] + _dot(t1[h].astype(BF16), nb[h])) for h in heads]

        eg = [jnp.exp(gcol[h]) for h in heads]
        rhs = [jnp.concatenate([v[h] * beta[h], kn[h] * (beta[h] * eg[h])], axis=1) for h in heads]
        uw = [rhs[h] + _dot(n[h].astype(BF16), rhs[h].astype(BF16)) for h in heads]
        aqk = [(both[h][c:2 * c] * decay[h]).astype(BF16) for h in heads]
        glast = [gcol[h][c - 1:c, :] for h in heads]
        lhs = [jnp.concatenate([uw[h][:, HEAD_DIM:2 * HEAD_DIM].astype(BF16),
                                (qn[h] * eg[h]).astype(BF16)], axis=0) for h in heads]
        kd = [(kn[h] * jnp.exp(glast[h] - gcol[h])).astype(BF16) for h in heads]

        s = [s_ref[h] for h in heads]
        ws = [_dot(lhs[h], s[h].astype(BF16)) for h in heads]
        vnb = [(uw[h][:, 0:HEAD_DIM] - ws[h][0:c]).astype(BF16) for h in heads]
        o = [ws[h][c:2 * c] + _dot(aqk[h], vnb[h]) for h in heads]
        s_new = [jnp.exp(glast[h]) * s[h] + _dot_tn(kd[h], vnb[h]) for h in heads]
        for h in heads:
            s_ref[h] = s_new[h]
        o_all = jnp.concatenate([_rms(o[h], onw) for h in heads], axis=1)
        o_ref[rows, :] = o_all * (z_all * _sigmoid(z_all))
        return carry

    lax.fori_loop(0, n_sub, body, 0)

    @pl.when(step == pl.num_programs(1) - 1)
    def _():
        sout_ref[...] = s_ref[...]


def _gdn_scan(proj, conv_w, a_log, dt_bias, out_norm, s0, buf, chunk, block, t_valid):
    bsz, t, width = proj.shape
    has_state = s0 is not None
    tri = jnp.asarray(np.tril(np.ones((chunk, chunk), np.float32)), dtype=BF16)

    def const(shape):
        return pl.BlockSpec(shape, lambda b, s: (0, 0))

    def head_row(x):
        return jnp.zeros((1, LANES), F32).at[0, 0:N_HEADS].set(x)

    state_spec = pl.BlockSpec((None, N_HEADS, HEAD_DIM, HEAD_DIM), lambda b, s: (b, 0, 0, 0))
    in_specs = [pl.BlockSpec((None, block, width), lambda b, s: (b, s, 0)),
                const(conv_w.shape), const((1, LANES)), const((1, LANES)), const((1, HEAD_DIM)),
                const(tri.shape)]
    args = [proj, conv_w, head_row(a_log), head_row(dt_bias), out_norm.reshape(1, HEAD_DIM), tri]
    if has_state:
        in_specs += [state_spec,
                     pl.BlockSpec((None, GDN_CONV - 1, GDN_CONV_DIM), lambda b, s: (b, 0, 0))]
        args += [s0, buf]
    kern = functools.partial(_gdn_kernel, chunk=chunk, n_sub=block // chunk, block=block,
                             t_valid=t_valid, has_state=has_state)
    return pl.pallas_call(
        kern,
        grid=(bsz, t // block),
        in_specs=in_specs,
        out_specs=[pl.BlockSpec((None, block, D_MODEL), lambda b, s: (b, s, 0)), state_spec],
        out_shape=[jax.ShapeDtypeStruct((bsz, t, D_MODEL), F32),
                   jax.ShapeDtypeStruct((bsz, N_HEADS, HEAD_DIM, HEAD_DIM), F32)],
        scratch_shapes=[pltpu.VMEM((N_HEADS, HEAD_DIM, HEAD_DIM), F32),
                        pltpu.VMEM((block + 2 * SUBLANES, GDN_CONV_DIM), F32),
                        pltpu.VMEM((block, GDN_CONV_DIM), F32)],
        compiler_params=pltpu.CompilerParams(
            dimension_semantics=("arbitrary", "arbitrary"), vmem_limit_bytes=VMEM_LIMIT),
        name="gdn_scan",
    )(*args)


def kernel(x_prompt, x_sample, state_hgrn, state_gdn, state_gdn_conv, norm_mix, norm_mlp, norm_final,
           hgrn_w_in, hgrn_lb, hgrn_out_norm, hgrn_w_out, gdn_w_in, gdn_conv_w, gdn_A_log,
           gdn_dt_bias, gdn_out_norm, gdn_w_out, mlp_w_up, mlp_w_down):
    depth = norm_mix.shape[0]
    pb, pt, d = x_prompt.shape
    sb, st, _ = x_sample.shape
    assert pt % PROMPT_BLOCK == 0 and st <= SAMPLE_T_PAD and st >= GDN_CONV - 1

    groups = [
        dict(b=pb, t=pt, tv=PROMPT_CHUNK, chunk=PROMPT_CHUNK, block=PROMPT_BLOCK, x=x_prompt),
        dict(b=sb, t=SAMPLE_T_PAD, tv=st, chunk=SAMPLE_T_PAD, block=SAMPLE_T_PAD,
             x=jnp.pad(x_sample, ((0, 0), (0, SAMPLE_T_PAD - st), (0, 0)))),
    ]
    states = [dict(h=None, g=None, c=None),
              dict(h=state_hgrn, g=state_gdn, c=state_gdn_conv)]
    xs = [g["x"].reshape(g["b"] * g["t"], d) for g in groups]
    new_h, new_g, new_c = ([], []), ([], []), ([], [])
    ys = [None, None]

    for i in range(depth):
        j = i // 2
        last = i == depth - 1
        w_out = (hgrn_w_out if i % 2 == 0 else gdn_w_out)[j].astype(BF16)
        w_up = mlp_w_up[i].astype(BF16)
        w_down = mlp_w_down[i].astype(BF16)
        if i % 2 == 0:
            w_in = hgrn_w_in[j].astype(BF16)
        else:
            w_in = jnp.pad(gdn_w_in[j], ((0, 0), (0, GDN_IN_PAD - gdn_w_in.shape[2]))).astype(BF16)
        for gi, (g, s) in enumerate(zip(groups, states)):
            proj = _rms_matmul(xs[gi], norm_mix[i], w_in).reshape(g["b"], g["t"], -1)
            if i % 2 == 0:
                o, s_new = _hgrn_scan(proj, hgrn_lb, hgrn_out_norm[j],
                                      None if s["h"] is None else s["h"][j],
                                      j, g["chunk"], g["block"], g["tv"])
                new_h[gi].append(s_new)
            else:
                o, s_new = _gdn_scan(proj, gdn_conv_w[j], gdn_A_log[j], gdn_dt_bias[j],
                                     gdn_out_norm[j],
                                     None if s["g"] is None else s["g"][j],
                                     None if s["c"] is None else s["c"][j],
                                     g["chunk"], g["block"], g["tv"])
                new_g[gi].append(s_new)
                tv = g["tv"] if gi == 1 else g["t"]
                new_c[gi].append(proj[:, tv - (GDN_CONV - 1):tv, 0:GDN_CONV_DIM])
            res = _out_mlp(o.reshape(-1, d), xs[gi], w_out, norm_mlp[i], w_up, w_down,
                           norm_final, last)
            xs[gi] = res[0]
            if last:
                ys[gi] = res[1]

    y_prompt = ys[0].reshape(pb, pt, d)
    y_sample = ys[1].reshape(sb, SAMPLE_T_PAD, d)[:, 0:st]
    return (y_prompt, y_sample, jnp.stack(new_h[0]), jnp.stack(new_h[1]),
            jnp.stack(new_g[0]), jnp.stack(new_g[1]), jnp.stack(new_c[0]), jnp.stack(new_c[1]))
```

```python
import functools

import jax
import jax.numpy as jnp
import numpy as np
from jax import lax
from jax.experimental import pallas as pl
from jax.experimental.pallas import tpu as pltpu

F32 = jnp.float32
BF16 = jnp.bfloat16

D_MODEL = 1024
N_HEADS = 8
HEAD_DIM = 128
D_FF = 4 * D_MODEL
FF_CHUNK = 1024
GDN_CONV = 4
GDN_CONV_DIM = 3 * D_MODEL
GDN_Z_COL = GDN_CONV_DIM
GDN_GB_COL = GDN_CONV_DIM + D_MODEL
GDN_IN_PAD = GDN_GB_COL + HEAD_DIM
EPS = 1e-6
LB_FLOOR = 1e-30
LANES = 128
SUBLANES = 8
ROW_TILE = 512
PROMPT_CHUNK = 64
PROMPT_BLOCK = 256
CHUNK_UNROLL = 2
SAMPLE_T_PAD = 8
VMEM_LIMIT = 56 * 1024 * 1024


def _dot(a, b):
    return jnp.dot(a, b, preferred_element_type=F32)


def _dot_nt(a, b):
    return lax.dot_general(a, b, (((1,), (1,)), ((), ())), preferred_element_type=F32)


def _dot_tn(a, b):
    return lax.dot_general(a, b, (((0,), (0,)), ((), ())), preferred_element_type=F32)


def _dot_table(table, x):
    hi = x.astype(BF16)
    r = x - hi.astype(F32)
    mid = r.astype(BF16)
    lo = (r - mid.astype(F32)).astype(BF16)
    return _dot(table, hi) + _dot(table, mid) + _dot(table, lo)


def _sigmoid(x):
    return 1.0 / (1.0 + jnp.exp(-x))


def _softplus(x):
    return jnp.maximum(x, 0.0) + jnp.log1p(jnp.exp(-jnp.abs(x)))


def _rms(x, w):
    return x * lax.rsqrt(jnp.mean(x * x, axis=-1, keepdims=True) + EPS) * w


def _level_masks(c):
    tt = lax.broadcasted_iota(jnp.int32, (c, c), 0)
    ss = lax.broadcasted_iota(jnp.int32, (c, c), 1)
    levels = int(np.log2(c))
    assert 2 ** levels == c
    return [(((tt >> l) & 1) == 1) & (((ss >> l) & 1) == 0) & ((tt >> (l + 1)) == (ss >> (l + 1)))
            for l in range(levels)], tt, ss


def _rms_matmul_kernel(x_ref, nw_ref, w_ref, *rest):
    h = _rms(x_ref[...], nw_ref[...]).astype(BF16)
    if len(rest) == 1:
        rest[0][...] = _dot(h, w_ref[...])
    else:
        w2_ref, o_ref = rest
        n = w_ref.shape[1]
        o_ref[:, 0:n] = _dot(h, w_ref[...])
        o_ref[:, n:] = _dot(h, w2_ref[...])


def _rms_matmul(x, nw, w, w2=None):
    m, d = x.shape
    n = w.shape[1] + (0 if w2 is None else w2.shape[1])
    tm = min(ROW_TILE, m)

    def whole(shape):
        return pl.BlockSpec(shape, lambda i: (0, 0), pipeline_mode=pl.Buffered(1))

    in_specs = [pl.BlockSpec((tm, d), lambda i: (i, 0)), pl.BlockSpec((1, d), lambda i: (0, 0)),
                whole(w.shape)]
    args = [x, nw.reshape(1, d), w]
    if w2 is not None:
        in_specs.append(whole(w2.shape))
        args.append(w2)
    return pl.pallas_call(
        _rms_matmul_kernel,
        grid=(m // tm,),
        in_specs=in_specs,
        out_specs=pl.BlockSpec((tm, n), lambda i: (i, 0)),
        out_shape=jax.ShapeDtypeStruct((m, n), F32),
        compiler_params=pltpu.CompilerParams(
            dimension_semantics=("arbitrary",), vmem_limit_bytes=VMEM_LIMIT),
        name="rms_matmul",
    )(*args)


def _out_mlp_kernel(o_ref, x_ref, wo_ref, nw_ref, wu_ref, wd_ref, nf_ref, xo_ref, y_ref=None):
    x1 = x_ref[...] + _dot(o_ref[...].astype(BF16), wo_ref[...])
    h = _rms(x1, nw_ref[...]).astype(BF16)
    acc = x1
    for c in range(D_FF // FF_CHUNK):
        u = _dot(h, wu_ref[:, c * FF_CHUNK:(c + 1) * FF_CHUNK])
        a = jnp.square(jnp.maximum(u, 0.0)).astype(BF16)
        acc = acc + _dot(a, wd_ref[c * FF_CHUNK:(c + 1) * FF_CHUNK, :])
    xo_ref[...] = acc
    if y_ref is not None:
        y_ref[...] = _rms(acc, nf_ref[...])


def _out_mlp(o, x, w_out, nw, w_up, w_down, nf, final):
    m, d = x.shape
    tm = min(ROW_TILE, m)
    row = pl.BlockSpec((tm, d), lambda i: (i, 0))
    vec = pl.BlockSpec((1, d), lambda i: (0, 0))

    def whole(shape):
        return pl.BlockSpec(shape, lambda i: (0, 0), pipeline_mode=pl.Buffered(1))

    out_shape = [jax.ShapeDtypeStruct((m, d), F32)]
    out_specs = [row]
    if final:
        out_shape.append(jax.ShapeDtypeStruct((m, d), F32))
        out_specs.append(row)
    return pl.pallas_call(
        _out_mlp_kernel,
        grid=(m // tm,),
        in_specs=[row, row, whole(w_out.shape), vec, whole(w_up.shape), whole(w_down.shape), vec],
        out_specs=out_specs,
        out_shape=out_shape,
        compiler_params=pltpu.CompilerParams(
            dimension_semantics=("arbitrary",), vmem_limit_bytes=VMEM_LIMIT),
        name="out_mlp",
    )(o, x, w_out, nw.reshape(1, d), w_up, w_down, nf.reshape(1, d))


def _hgrn_tables(c):
    levels = int(np.log2(c))
    t = np.arange(c)[:, None]
    r = np.arange(c)[None, :]
    summing = [(r <= t), (r > t)]
    for l in range(levels):
        h = 2 ** l
        start = (t // (2 * h)) * (2 * h)
        upper = (t - start) >= h
        ref = start + h - 1
        summing.append(np.where(upper, (r > ref) & (r <= t), (r > t) & (r <= ref)))
    return np.concatenate(summing, 0).astype(np.float32)


def _hgrn_kernel(*refs, layer, chunk, n_sub, unroll, t_valid, has_state):
    if has_state:
        p_ref, lb_ref, on_ref, sum_ref, s0_ref, o_ref, sout_ref, st_ref = refs
    else:
        p_ref, lb_ref, on_ref, sum_ref, o_ref, sout_ref, st_ref = refs
    step = pl.program_id(1)
    c = chunk
    d = D_MODEL

    @pl.when(step == 0)
    def _():
        for h in range(N_HEADS):
            if has_state:
                st_ref[h] = s0_ref[h].T
            else:
                st_ref[h] = jnp.zeros((HEAD_DIM, HEAD_DIM), F32)

    lbraw = lb_ref[...]
    e = jnp.exp(lbraw - jnp.max(lbraw, axis=0, keepdims=True))
    sm = e / jnp.sum(e, axis=0, keepdims=True)
    lb = sm[0:1, :]
    for i in range(1, layer + 1):
        lb = lb + sm[i:i + 1, :]
    lb = lb - sm[0:1, :]
    lb_floor = jnp.maximum(lb, LB_FLOOR)
    one_m_lb = 1.0 - lb

    masks, tt, ss = _level_masks(c)
    diag = tt == ss
    row = lax.broadcasted_iota(jnp.int32, (c, HEAD_DIM), 0)
    uppers = [((row >> l) & 1) == 1 for l in range(len(masks))]
    onw = on_ref[...]
    table = sum_ref[...]

    def body(j, carry):
        subs = range(unroll)
        heads = range(N_HEADS)
        units = [(u, h) for u in subs for h in heads]
        idx = range(len(units))
        cs = [slice(h * HEAD_DIM, (h + 1) * HEAD_DIM) for h in heads]
        rows = [pl.ds(pl.multiple_of((j * unroll + u) * c, c), c) for u in subs]
        q_all, k_all, v_all, gate_all, ex_all = [], [], [], [], []
        for u in subs:
            qr = p_ref[rows[u], 0:d]
            fz = p_ref[rows[u], d:2 * d]
            v_u = p_ref[rows[u], 2 * d:3 * d]
            gr = p_ref[rows[u], 3 * d:4 * d]
            q_u = qr * _sigmoid(qr)
            gate_all.append(gr * _sigmoid(gr))
            ez = jnp.exp(-jnp.abs(fz))
            rz = 1.0 / (1.0 + ez)
            pos = fz >= 0.0
            lf_u = jnp.log(lb_floor + one_m_lb * jnp.where(pos, rz, ez * rz))
            k_u = one_m_lb * jnp.where(pos, ez * rz, rz)
            if t_valid < c:
                live = lax.broadcasted_iota(jnp.int32, (c, d), 0) < t_valid
                q_u = jnp.where(live, q_u, 0.0)
                k_u = jnp.where(live, k_u, 0.0)
                v_u = jnp.where(live, v_u, 0.0)
                lf_u = jnp.where(live, lf_u, 0.0)
            q_all.append(q_u)
            k_all.append(k_u)
            v_all.append(v_u)
            ex_all.append(jnp.exp(_dot_table(table, lf_u)))

        q = [q_all[u][:, cs[h]] for u, h in units]
        k = [k_all[u][:, cs[h]] for u, h in units]
        vb = [v_all[u][:, cs[h]].astype(BF16) for u, h in units]

        def ex(i, lo, hi):
            u, h = units[i]
            return ex_all[u][lo:hi, cs[h]]

        a = [jnp.where(diag, _dot_nt(q[i].astype(BF16), k[i].astype(BF16)), 0.0) for i in idx]
        for l, (m, upper) in enumerate(zip(masks, uppers)):
            x = [(jnp.where(upper, q[i], k[i]) * ex(i, (2 + l) * c, (3 + l) * c)).astype(BF16)
                 for i in idx]
            a = [a[i] + jnp.where(m, _dot_nt(x[i], x[i]), 0.0) for i in idx]
        o_intra = [_dot(a[i].astype(BF16), vb[i]) for i in idx]
        qd = [(q[i] * ex(i, 0, c)).astype(BF16) for i in idx]
        kh = [(k[i] * ex(i, c, 2 * c)).astype(BF16) for i in idx]
        kv = [_dot_tn(vb[i], kh[i]) for i in idx]

        st = [st_ref[h] for h in heads]
        for u in subs:
            ix = [u * N_HEADS + h for h in heads]
            o = [o_intra[ix[h]] + _dot_nt(qd[ix[h]], st[h].astype(BF16)) for h in heads]
            st = [st[h] * ex(ix[h], c - 1, c) + kv[ix[h]] for h in heads]
            o_all = jnp.concatenate([_rms(o[h], onw) for h in heads], axis=1)
            o_ref[rows[u], :] = o_all * gate_all[u]
        for h in heads:
            st_ref[h] = st[h]
        return carry

    lax.fori_loop(0, n_sub // unroll, body, 0)

    @pl.when(step == pl.num_programs(1) - 1)
    def _():
        for h in range(N_HEADS):
            sout_ref[h] = st_ref[h].T


def _hgrn_scan(proj, lb_raw, out_norm, s0, layer, chunk, block, t_valid):
    bsz, t, width = proj.shape
    table = jnp.asarray(_hgrn_tables(chunk), dtype=BF16)
    has_state = s0 is not None

    def const(shape):
        return pl.BlockSpec(shape, lambda b, s: (0, 0))

    state_spec = pl.BlockSpec((None, N_HEADS, HEAD_DIM, HEAD_DIM), lambda b, s: (b, 0, 0, 0))
    in_specs = [pl.BlockSpec((None, block, width), lambda b, s: (b, s, 0)),
                const(lb_raw.shape), const((1, HEAD_DIM)), const(table.shape)]
    args = [proj, lb_raw, out_norm.reshape(1, HEAD_DIM), table]
    if has_state:
        in_specs.append(state_spec)
        args.append(s0)
    n_sub = block // chunk
    kern = functools.partial(_hgrn_kernel, layer=layer, chunk=chunk, n_sub=n_sub,
                             unroll=min(CHUNK_UNROLL, n_sub), t_valid=t_valid,
                             has_state=has_state)
    return pl.pallas_call(
        kern,
        grid=(bsz, t // block),
        in_specs=in_specs,
        out_specs=[pl.BlockSpec((None, block, D_MODEL), lambda b, s: (b, s, 0)), state_spec],
        out_shape=[jax.ShapeDtypeStruct((bsz, t, D_MODEL), F32),
                   jax.ShapeDtypeStruct((bsz, N_HEADS, HEAD_DIM, HEAD_DIM), F32)],
        scratch_shapes=[pltpu.VMEM((N_HEADS, HEAD_DIM, HEAD_DIM), F32)],
        compiler_params=pltpu.CompilerParams(
            dimension_semantics=("arbitrary", "arbitrary"), vmem_limit_bytes=VMEM_LIMIT),
        name="hgrn_scan",
    )(*args)


def _gdn_kernel(*refs, chunk, n_sub, unroll, block, t_valid, has_state):
    if has_state:
        (p_ref, cw_ref, al_ref, dt_ref, on_ref, tri_ref, s0_ref, buf_ref,
         o_ref, sout_ref, s_ref, xs, ys) = refs
    else:
        (p_ref, cw_ref, al_ref, dt_ref, on_ref, tri_ref,
         o_ref, sout_ref, s_ref, xs, ys) = refs
    step = pl.program_id(1)
    c = chunk
    d = D_MODEL
    tail = SUBLANES
    taps = GDN_CONV - 1

    @pl.when(step == 0)
    def _():
        if has_state:
            s_ref[...] = s0_ref[...]
            xs[tail - taps:tail, :] = buf_ref[...]
        else:
            s_ref[...] = jnp.zeros_like(s_ref)
            xs[0:tail, :] = jnp.zeros((tail, GDN_CONV_DIM), F32)

    xs[tail:tail + block, :] = p_ref[:, 0:GDN_CONV_DIM]
    y = xs[pl.ds(tail - taps, block), :] * cw_ref[0:1, :]
    for j in range(1, GDN_CONV):
        y = y + xs[pl.ds(tail - taps + j, block), :] * cw_ref[j:j + 1, :]
    ys[...] = y * _sigmoid(y)
    xs[0:tail, :] = xs[block:block + tail, :]

    masks, tt, ss = _level_masks(c)
    incl = ss <= tt
    strict = ss < tt
    onw = on_ref[...]
    neg_a = -jnp.exp(al_ref[...])
    dtb = dt_ref[...]
    tri = tri_ref[...]

    def body(j, carry):
        subs = range(unroll)
        heads = range(N_HEADS)
        units = [(u, h) for u in subs for h in heads]
        rows = [pl.ds(pl.multiple_of((j * unroll + u) * c, c), c) for u in subs]
        gcum, gcum_t, beta_all, qkv, z_all = [], [], [], [], []
        for u in subs:
            gb = p_ref[rows[u], GDN_GB_COL:GDN_GB_COL + LANES]
            g_u = neg_a * _softplus(gb + dtb)
            beta_u = _sigmoid(gb)
            if t_valid < c:
                live = lax.broadcasted_iota(jnp.int32, (c, LANES), 0) < t_valid
                g_u = jnp.where(live, g_u, 0.0)
                beta_u = jnp.where(live, beta_u, 0.0)
            gc_u = _dot_table(tri, g_u)
            gcum.append(gc_u)
            gcum_t.append(jnp.concatenate([gc_u, jnp.zeros((LANES - c, LANES), F32)], axis=0).T)
            beta_all.append(beta_u)
            qkv.append(ys[rows[u], :])
            z_all.append(p_ref[rows[u], GDN_Z_COL:GDN_Z_COL + d])

        qn, kn, v, gcol, beta, decay = [], [], [], [], [], []
        for u, h in units:
            qc = qkv[u][:, h * HEAD_DIM:(h + 1) * HEAD_DIM]
            kc = qkv[u][:, d + h * HEAD_DIM:d + (h + 1) * HEAD_DIM]
            vh = qkv[u][:, 2 * d + h * HEAD_DIM:2 * d + (h + 1) * HEAD_DIM]
            qh = qc * lax.rsqrt(jnp.sum(qc * qc, axis=-1, keepdims=True) + EPS) * (HEAD_DIM ** -0.5)
            kh = kc * lax.rsqrt(jnp.sum(kc * kc, axis=-1, keepdims=True) + EPS)
            if t_valid < c:
                qh = jnp.where(live, qh, 0.0)
                kh = jnp.where(live, kh, 0.0)
                vh = jnp.where(live, vh, 0.0)
            qn.append(qh)
            kn.append(kh)
            v.append(vh)
            gc = jnp.broadcast_to(gcum[u][:, h:h + 1], (c, LANES))
            grow = jnp.broadcast_to(gcum_t[u][h:h + 1, 0:c], (c, c))
            gcol.append(gc)
            beta.append(jnp.broadcast_to(beta_all[u][:, N_HEADS + h:N_HEADS + h + 1], (c, LANES)))
            decay.append(jnp.where(incl, jnp.exp(jnp.where(incl, gc[:, 0:c] - grow, 0.0)), 0.0))

        idx = range(len(units))
        kb = [kn[i].astype(BF16) for i in idx]
        both = [_dot_nt(jnp.concatenate([kb[i], qn[i].astype(BF16)], axis=0), kb[i])
                for i in idx]
        low = [jnp.where(strict, beta[i][:, 0:c] * both[i][0:c] * decay[i], 0.0) for i in idx]
        n = [-jnp.where(masks[0], low[i], 0.0) for i in idx]
        for m in masks[1:]:
            cl = [jnp.where(m, low[i], 0.0) for i in idx]
            nb = [n[i].astype(BF16) for i in idx]
            t1 = [cl[i] + _dot(nb[i], cl[i].astype(BF16)) for i in idx]
            n = [n[i] - (t1[i] + _dot(t1[i].astype(BF16), nb[i])) for i in idx]

        eg = [jnp.exp(gcol[i]) for i in idx]
        rhs = [jnp.concatenate([v[i] * beta[i], kn[i] * (beta[i] * eg[i])], axis=1) for i in idx]
        uw = [rhs[i] + _dot(n[i].astype(BF16), rhs[i].astype(BF16)) for i in idx]
        aqk = [(both[i][c:2 * c] * decay[i]).astype(BF16) for i in idx]
        glast = [gcol[i][c - 1:c, :] for i in idx]
        lhs = [jnp.concatenate([uw[i][:, HEAD_DIM:2 * HEAD_DIM].astype(BF16),
                                (qn[i] * eg[i]).astype(BF16)], axis=0) for i in idx]
        kd = [(kn[i] * jnp.exp(glast[i] - gcol[i])).astype(BF16) for i in idx]

        s = [s_ref[h] for h in heads]
        for u in subs:
            ix = [u * N_HEADS + h for h in heads]
            ws = [_dot(lhs[ix[h]], s[h].astype(BF16)) for h in heads]
            vnb = [(uw[ix[h]][:, 0:HEAD_DIM] - ws[h][0:c]).astype(BF16) for h in heads]
            o = [ws[h][c:2 * c] + _dot(aqk[ix[h]], vnb[h]) for h in heads]
            s = [jnp.exp(glast[ix[h]]) * s[h] + _dot_tn(kd[ix[h]], vnb[h]) for h in heads]
            o_all = jnp.concatenate([_rms(o[h], onw) for h in heads], axis=1)
            o_ref[rows[u], :] = o_all * (z_all[u] * _sigmoid(z_all[u]))
        for h in heads:
            s_ref[h] = s[h]
        return carry

    lax.fori_loop(0, n_sub // unroll, body, 0)

    @pl.when(step == pl.num_programs(1) - 1)
    def _():
        sout_ref[...] = s_ref[...]


def _gdn_scan(proj, conv_w, a_log, dt_bias, out_norm, s0, buf, chunk, block, t_valid):
    bsz, t, width = proj.shape
    has_state = s0 is not None
    tri = jnp.asarray(np.tril(np.ones((chunk, chunk), np.float32)), dtype=BF16)

    def const(shape):
        return pl.BlockSpec(shape, lambda b, s: (0, 0))

    def head_row(x):
        return jnp.zeros((1, LANES), F32).at[0, 0:N_HEADS].set(x)

    state_spec = pl.BlockSpec((None, N_HEADS, HEAD_DIM, HEAD_DIM), lambda b, s: (b, 0, 0, 0))
    in_specs = [pl.BlockSpec((None, block, width), lambda b, s: (b, s, 0)),
                const(conv_w.shape), const((1, LANES)), const((1, LANES)), const((1, HEAD_DIM)),
                const(tri.shape)]
    args = [proj, conv_w, head_row(a_log), head_row(dt_bias), out_norm.reshape(1, HEAD_DIM), tri]
    if has_state:
        in_specs += [state_spec,
                     pl.BlockSpec((None, GDN_CONV - 1, GDN_CONV_DIM), lambda b, s: (b, 0, 0))]
        args += [s0, buf]
    n_sub = block // chunk
    kern = functools.partial(_gdn_kernel, chunk=chunk, n_sub=n_sub,
                             unroll=min(2 * CHUNK_UNROLL, n_sub), block=block,
                             t_valid=t_valid, has_state=has_state)
    return pl.pallas_call(
        kern,
        grid=(bsz, t // block),
        in_specs=in_specs,
        out_specs=[pl.BlockSpec((None, block, D_MODEL), lambda b, s: (b, s, 0)), state_spec],
        out_shape=[jax.ShapeDtypeStruct((bsz, t, D_MODEL), F32),
                   jax.ShapeDtypeStruct((bsz, N_HEADS, HEAD_DIM, HEAD_DIM), F32)],
        scratch_shapes=[pltpu.VMEM((N_HEADS, HEAD_DIM, HEAD_DIM), F32),
                        pltpu.VMEM((block + 2 * SUBLANES, GDN_CONV_DIM), F32),
                        pltpu.VMEM((block, GDN_CONV_DIM), F32)],
        compiler_params=pltpu.CompilerParams(
            dimension_semantics=("arbitrary", "arbitrary"), vmem_limit_bytes=VMEM_LIMIT),
        name="gdn_scan",
    )(*args)


def kernel(x_prompt, x_sample, state_hgrn, state_gdn, state_gdn_conv, norm_mix, norm_mlp, norm_final,
           hgrn_w_in, hgrn_lb, hgrn_out_norm, hgrn_w_out, gdn_w_in, gdn_conv_w, gdn_A_log,
           gdn_dt_bias, gdn_out_norm, gdn_w_out, mlp_w_up, mlp_w_down):
    depth = norm_mix.shape[0]
    pb, pt, d = x_prompt.shape
    sb, st, _ = x_sample.shape
    assert pt % PROMPT_BLOCK == 0 and st <= SAMPLE_T_PAD and st >= GDN_CONV - 1

    groups = [
        dict(b=pb, t=pt, tv=PROMPT_CHUNK, chunk=PROMPT_CHUNK, block=PROMPT_BLOCK, x=x_prompt),
        dict(b=sb, t=SAMPLE_T_PAD, tv=st, chunk=SAMPLE_T_PAD, block=SAMPLE_T_PAD,
             x=jnp.pad(x_sample, ((0, 0), (0, SAMPLE_T_PAD - st), (0, 0)))),
    ]
    states = [dict(h=None, g=None, c=None),
              dict(h=state_hgrn, g=state_gdn, c=state_gdn_conv)]
    xs = [g["x"].reshape(g["b"] * g["t"], d) for g in groups]
    new_h, new_g, new_c = ([], []), ([], []), ([], [])
    ys = [None, None]

    for i in range(depth):
        j = i // 2
        last = i == depth - 1
        w_out = (hgrn_w_out if i % 2 == 0 else gdn_w_out)[j].astype(BF16)
        w_up = mlp_w_up[i].astype(BF16)
        w_down = mlp_w_down[i].astype(BF16)
        if i % 2 == 0:
            w_in = hgrn_w_in[j].astype(BF16)
            w_ab = None
        else:
            w_in = gdn_w_in[j][:, 0:GDN_GB_COL].astype(BF16)
            w_ab = gdn_w_in[j][:, GDN_GB_COL:]
            w_ab = jnp.pad(w_ab, ((0, 0), (0, LANES - w_ab.shape[1]))).astype(BF16)
        for gi, (g, s) in enumerate(zip(groups, states)):
            proj = _rms_matmul(xs[gi], norm_mix[i], w_in, w_ab).reshape(g["b"], g["t"], -1)
            if i % 2 == 0:
                o, s_new = _hgrn_scan(proj, hgrn_lb, hgrn_out_norm[j],
                                      None if s["h"] is None else s["h"][j],
                                      j, g["chunk"], g["block"], g["tv"])
                new_h[gi].append(s_new)
            else:
                o, s_new = _gdn_scan(proj, gdn_conv_w[j], gdn_A_log[j], gdn_dt_bias[j],
                                     gdn_out_norm[j],
                                     None if s["g"] is None else s["g"][j],
                                     None if s["c"] is None else s["c"][j],
                                     g["chunk"], g["block"], g["tv"])
                new_g[gi].append(s_new)
                tv = g["tv"] if gi == 1 else g["t"]
                new_c[gi].append(proj[:, tv - (GDN_CONV - 1):tv, 0:GDN_CONV_DIM])
            res = _out_mlp(o.reshape(-1, d), xs[gi], w_out, norm_mlp[i], w_up, w_down,
                           norm_final, last)
            xs[gi] = res[0]
            if last:
                ys[gi] = res[1]

    y_prompt = ys[0].reshape(pb, pt, d)
    y_sample = ys[1].reshape(sb, SAMPLE_T_PAD, d)[:, 0:st]
    return (y_prompt, y_sample, jnp.stack(new_h[0]), jnp.stack(new_h[1]),
            jnp.stack(new_g[0]), jnp.stack(new_g[1]), jnp.stack(new_c[0]), jnp.stack(new_c[1]))
```

```python
import functools

import jax
import jax.numpy as jnp
import numpy as np
from jax import lax
from jax.experimental import pallas as pl
from jax.experimental.pallas import tpu as pltpu

F32 = jnp.float32
BF16 = jnp.bfloat16

D_MODEL = 1024
N_HEADS = 8
HEAD_DIM = 128
D_FF = 4 * D_MODEL
FF_CHUNK = 1024
GDN_CONV = 4
GDN_CONV_DIM = 3 * D_MODEL
GDN_Z_COL = GDN_CONV_DIM
GDN_GB_COL = GDN_CONV_DIM + D_MODEL
GDN_IN_PAD = GDN_GB_COL + HEAD_DIM
EPS = 1e-6
LB_FLOOR = 1e-30
LANES = 128
SUBLANES = 8
ROW_TILE = 512
PROMPT_CHUNK = 64
PROMPT_BLOCK = 256
CHUNK_UNROLL = 2
SAMPLE_T_PAD = 8
VMEM_LIMIT = 56 * 1024 * 1024


def _dot(a, b):
    return jnp.dot(a, b, preferred_element_type=F32)


def _dot_nt(a, b):
    return lax.dot_general(a, b, (((1,), (1,)), ((), ())), preferred_element_type=F32)


def _dot_tn(a, b):
    return lax.dot_general(a, b, (((0,), (0,)), ((), ())), preferred_element_type=F32)


def _dot_table(table, x):
    hi = x.astype(BF16)
    r = x - hi.astype(F32)
    mid = r.astype(BF16)
    lo = (r - mid.astype(F32)).astype(BF16)
    return _dot(table, hi) + _dot(table, mid) + _dot(table, lo)


def _sigmoid(x):
    return 1.0 / (1.0 + jnp.exp(-x))


def _softplus(x):
    return jnp.maximum(x, 0.0) + jnp.log1p(jnp.exp(-jnp.abs(x)))


def _rms(x, w):
    return x * lax.rsqrt(jnp.mean(x * x, axis=-1, keepdims=True) + EPS) * w


def _level_masks(c):
    tt = lax.broadcasted_iota(jnp.int32, (c, c), 0)
    ss = lax.broadcasted_iota(jnp.int32, (c, c), 1)
    levels = int(np.log2(c))
    assert 2 ** levels == c
    return [(((tt >> l) & 1) == 1) & (((ss >> l) & 1) == 0) & ((tt >> (l + 1)) == (ss >> (l + 1)))
            for l in range(levels)], tt, ss


def _state_io(s0, layer, bsz, s0_arg):
    if s0 is None:
        spec = pl.BlockSpec((None, N_HEADS, HEAD_DIM, HEAD_DIM), lambda b, s: (b, 0, 0, 0))
        return spec, jax.ShapeDtypeStruct((bsz, N_HEADS, HEAD_DIM, HEAD_DIM), F32), {}
    spec = pl.BlockSpec((None, None, N_HEADS, HEAD_DIM, HEAD_DIM),
                        lambda b, s: (layer, b, 0, 0, 0))
    return spec, jax.ShapeDtypeStruct(s0.shape, F32), {s0_arg: 1}


def _rms_matmul_kernel(x_ref, nw_ref, w_ref, *rest):
    h = _rms(x_ref[...], nw_ref[...]).astype(BF16)
    if len(rest) == 1:
        rest[0][...] = _dot(h, w_ref[...])
    else:
        w2_ref, o_ref = rest
        n = w_ref.shape[1]
        o_ref[:, 0:n] = _dot(h, w_ref[...])
        o_ref[:, n:] = _dot(h, w2_ref[...])


def _rms_matmul(x, nw, w, w2=None):
    m, d = x.shape
    n = w.shape[1] + (0 if w2 is None else w2.shape[1])
    tm = min(ROW_TILE, m)

    def whole(shape):
        return pl.BlockSpec(shape, lambda i: (0, 0), pipeline_mode=pl.Buffered(1))

    in_specs = [pl.BlockSpec((tm, d), lambda i: (i, 0)), pl.BlockSpec((1, d), lambda i: (0, 0)),
                whole(w.shape)]
    args = [x, nw.reshape(1, d), w]
    if w2 is not None:
        in_specs.append(whole(w2.shape))
        args.append(w2)
    return pl.pallas_call(
        _rms_matmul_kernel,
        grid=(m // tm,),
        in_specs=in_specs,
        out_specs=pl.BlockSpec((tm, n), lambda i: (i, 0)),
        out_shape=jax.ShapeDtypeStruct((m, n), F32),
        compiler_params=pltpu.CompilerParams(
            dimension_semantics=("arbitrary",), vmem_limit_bytes=VMEM_LIMIT),
        name="rms_matmul",
    )(*args)


def _out_mlp_kernel(o_ref, x_ref, wo_ref, nw_ref, wu_ref, wd_ref, nf_ref, xo_ref, y_ref=None):
    x1 = x_ref[...] + _dot(o_ref[...].astype(BF16), wo_ref[...])
    h = _rms(x1, nw_ref[...]).astype(BF16)
    acc = x1
    for c in range(D_FF // FF_CHUNK):
        u = _dot(h, wu_ref[:, c * FF_CHUNK:(c + 1) * FF_CHUNK])
        a = jnp.square(jnp.maximum(u, 0.0)).astype(BF16)
        acc = acc + _dot(a, wd_ref[c * FF_CHUNK:(c + 1) * FF_CHUNK, :])
    xo_ref[...] = acc
    if y_ref is not None:
        y_ref[...] = _rms(acc, nf_ref[...])


def _out_mlp(o, x, w_out, nw, w_up, w_down, nf, final):
    m, d = x.shape
    tm = min(ROW_TILE, m)
    row = pl.BlockSpec((tm, d), lambda i: (i, 0))
    vec = pl.BlockSpec((1, d), lambda i: (0, 0))

    def whole(shape):
        return pl.BlockSpec(shape, lambda i: (0, 0), pipeline_mode=pl.Buffered(1))

    out_shape = [jax.ShapeDtypeStruct((m, d), F32)]
    out_specs = [row]
    if final:
        out_shape.append(jax.ShapeDtypeStruct((m, d), F32))
        out_specs.append(row)
    return pl.pallas_call(
        _out_mlp_kernel,
        grid=(m // tm,),
        in_specs=[row, row, whole(w_out.shape), vec, whole(w_up.shape), whole(w_down.shape), vec],
        out_specs=out_specs,
        out_shape=out_shape,
        compiler_params=pltpu.CompilerParams(
            dimension_semantics=("arbitrary",), vmem_limit_bytes=VMEM_LIMIT),
        name="out_mlp",
    )(o, x, w_out, nw.reshape(1, d), w_up, w_down, nf.reshape(1, d))


def _hgrn_tables(c):
    levels = int(np.log2(c))
    t = np.arange(c)[:, None]
    r = np.arange(c)[None, :]
    summing = [(r <= t), (r > t)]
    for l in range(levels):
        h = 2 ** l
        start = (t // (2 * h)) * (2 * h)
        upper = (t - start) >= h
        ref = start + h - 1
        summing.append(np.where(upper, (r > ref) & (r <= t), (r > t) & (r <= ref)))
    return np.concatenate(summing, 0).astype(np.float32)


def _hgrn_kernel(*refs, layer, chunk, n_sub, unroll, t_valid, has_state):
    if has_state:
        p_ref, lb_ref, on_ref, sum_ref, s0_ref, o_ref, sout_ref, st_ref = refs
    else:
        p_ref, lb_ref, on_ref, sum_ref, o_ref, sout_ref, st_ref = refs
    step = pl.program_id(1)
    c = chunk
    d = D_MODEL

    @pl.when(step == 0)
    def _():
        for h in range(N_HEADS):
            if has_state:
                st_ref[h] = s0_ref[h].T
            else:
                st_ref[h] = jnp.zeros((HEAD_DIM, HEAD_DIM), F32)

    lbraw = lb_ref[...]
    e = jnp.exp(lbraw - jnp.max(lbraw, axis=0, keepdims=True))
    sm = e / jnp.sum(e, axis=0, keepdims=True)
    lb = sm[0:1, :]
    for i in range(1, layer + 1):
        lb = lb + sm[i:i + 1, :]
    lb = lb - sm[0:1, :]
    lb_floor = jnp.maximum(lb, LB_FLOOR)
    one_m_lb = 1.0 - lb

    masks, tt, ss = _level_masks(c)
    diag = tt == ss
    row = lax.broadcasted_iota(jnp.int32, (c, HEAD_DIM), 0)
    uppers = [((row >> l) & 1) == 1 for l in range(len(masks))]
    onw = on_ref[...]
    table = sum_ref[...]

    def body(j, carry):
        subs = range(unroll)
        heads = range(N_HEADS)
        units = [(u, h) for u in subs for h in heads]
        idx = range(len(units))
        cs = [slice(h * HEAD_DIM, (h + 1) * HEAD_DIM) for h in heads]
        rows = [pl.ds(pl.multiple_of((j * unroll + u) * c, c), c) for u in subs]
        q_all, k_all, v_all, gate_all, ex_all = [], [], [], [], []
        for u in subs:
            qr = p_ref[rows[u], 0:d]
            fz = p_ref[rows[u], d:2 * d]
            v_u = p_ref[rows[u], 2 * d:3 * d]
            gr = p_ref[rows[u], 3 * d:4 * d]
            q_u = qr * _sigmoid(qr)
            gate_all.append(gr * _sigmoid(gr))
            ez = jnp.exp(-jnp.abs(fz))
            rz = 1.0 / (1.0 + ez)
            pos = fz >= 0.0
            lf_u = jnp.log(lb_floor + one_m_lb * jnp.where(pos, rz, ez * rz))
            k_u = one_m_lb * jnp.where(pos, ez * rz, rz)
            if t_valid < c:
                live = lax.broadcasted_iota(jnp.int32, (c, d), 0) < t_valid
                q_u = jnp.where(live, q_u, 0.0)
                k_u = jnp.where(live, k_u, 0.0)
                v_u = jnp.where(live, v_u, 0.0)
                lf_u = jnp.where(live, lf_u, 0.0)
            q_all.append(q_u)
            k_all.append(k_u)
            v_all.append(v_u)
            ex_all.append(jnp.exp(_dot_table(table, lf_u)))

        q = [q_all[u][:, cs[h]] for u, h in units]
        k = [k_all[u][:, cs[h]] for u, h in units]
        vb = [v_all[u][:, cs[h]].astype(BF16) for u, h in units]

        def ex(i, lo, hi):
            u, h = units[i]
            return ex_all[u][lo:hi, cs[h]]

        a = [jnp.where(diag, _dot_nt(q[i].astype(BF16), k[i].astype(BF16)), 0.0) for i in idx]
        for l, (m, upper) in enumerate(zip(masks, uppers)):
            x = [(jnp.where(upper, q[i], k[i]) * ex(i, (2 + l) * c, (3 + l) * c)).astype(BF16)
                 for i in idx]
            a = [a[i] + jnp.where(m, _dot_nt(x[i], x[i]), 0.0) for i in idx]
        o_intra = [_dot(a[i].astype(BF16), vb[i]) for i in idx]
        qd = [(q[i] * ex(i, 0, c)).astype(BF16) for i in idx]
        kh = [(k[i] * ex(i, c, 2 * c)).astype(BF16) for i in idx]
        kv = [_dot_tn(vb[i], kh[i]) for i in idx]

        st = [st_ref[h] for h in heads]
        for u in subs:
            ix = [u * N_HEADS + h for h in heads]
            o = [o_intra[ix[h]] + _dot_nt(qd[ix[h]], st[h].astype(BF16)) for h in heads]
            st = [st[h] * ex(ix[h], c - 1, c) + kv[ix[h]] for h in heads]
            o_all = jnp.concatenate([_rms(o[h], onw) for h in heads], axis=1)
            o_ref[rows[u], :] = o_all * gate_all[u]
        for h in heads:
            st_ref[h] = st[h]
        return carry

    lax.fori_loop(0, n_sub // unroll, body, 0)

    @pl.when(step == pl.num_programs(1) - 1)
    def _():
        for h in range(N_HEADS):
            sout_ref[h] = st_ref[h].T


def _hgrn_scan(proj, lb_raw, out_norm, s0, layer, chunk, block, t_valid):
    bsz, t, width = proj.shape
    table = jnp.asarray(_hgrn_tables(chunk), dtype=BF16)
    has_state = s0 is not None

    def const(shape):
        return pl.BlockSpec(shape, lambda b, s: (0, 0))

    in_specs = [pl.BlockSpec((None, block, width), lambda b, s: (b, s, 0)),
                const(lb_raw.shape), const((1, HEAD_DIM)), const(table.shape)]
    args = [proj, lb_raw, out_norm.reshape(1, HEAD_DIM), table]
    state_spec, state_shape, aliases = _state_io(s0, layer, bsz, len(args))
    if has_state:
        in_specs.append(state_spec)
        args.append(s0)
    n_sub = block // chunk
    kern = functools.partial(_hgrn_kernel, layer=layer, chunk=chunk, n_sub=n_sub,
                             unroll=min(CHUNK_UNROLL, n_sub), t_valid=t_valid,
                             has_state=has_state)
    return pl.pallas_call(
        kern,
        grid=(bsz, t // block),
        in_specs=in_specs,
        out_specs=[pl.BlockSpec((None, block, D_MODEL), lambda b, s: (b, s, 0)), state_spec],
        out_shape=[jax.ShapeDtypeStruct((bsz, t, D_MODEL), F32), state_shape],
        input_output_aliases=aliases,
        scratch_shapes=[pltpu.VMEM((N_HEADS, HEAD_DIM, HEAD_DIM), F32)],
        compiler_params=pltpu.CompilerParams(
            dimension_semantics=("arbitrary", "arbitrary"), vmem_limit_bytes=VMEM_LIMIT),
        name="hgrn_scan",
    )(*args)


def _gdn_kernel(*refs, chunk, n_sub, unroll, block, t_valid, has_state):
    if has_state:
        (p_ref, cw_ref, al_ref, dt_ref, on_ref, tri_ref, s0_ref, buf_ref,
         o_ref, sout_ref, s_ref, xs, ys) = refs
    else:
        (p_ref, cw_ref, al_ref, dt_ref, on_ref, tri_ref,
         o_ref, sout_ref, s_ref, xs, ys) = refs
    step = pl.program_id(1)
    c = chunk
    d = D_MODEL
    tail = SUBLANES
    taps = GDN_CONV - 1

    @pl.when(step == 0)
    def _():
        if has_state:
            s_ref[...] = s0_ref[...]
            xs[tail - taps:tail, :] = buf_ref[...]
        else:
            s_ref[...] = jnp.zeros_like(s_ref)
            xs[0:tail, :] = jnp.zeros((tail, GDN_CONV_DIM), F32)

    xs[tail:tail + block, :] = p_ref[:, 0:GDN_CONV_DIM]
    y = xs[pl.ds(tail - taps, block), :] * cw_ref[0:1, :]
    for j in range(1, GDN_CONV):
        y = y + xs[pl.ds(tail - taps + j, block), :] * cw_ref[j:j + 1, :]
    ys[...] = y * _sigmoid(y)
    xs[0:tail, :] = xs[block:block + tail, :]

    masks, tt, ss = _level_masks(c)
    incl = ss <= tt
    strict = ss < tt
    onw = on_ref[...]
    neg_a = -jnp.exp(al_ref[...])
    dtb = dt_ref[...]
    tri = tri_ref[...]

    def body(j, carry):
        subs = range(unroll)
        heads = range(N_HEADS)
        units = [(u, h) for u in subs for h in heads]
        rows = [pl.ds(pl.multiple_of((j * unroll + u) * c, c), c) for u in subs]
        gcum, gcum_t, beta_all, qkv, z_all = [], [], [], [], []
        for u in subs:
            gb = p_ref[rows[u], GDN_GB_COL:GDN_GB_COL + LANES]
            g_u = neg_a * _softplus(gb + dtb)
            beta_u = _sigmoid(gb)
            if t_valid < c:
                live = lax.broadcasted_iota(jnp.int32, (c, LANES), 0) < t_valid
                g_u = jnp.where(live, g_u, 0.0)
                beta_u = jnp.where(live, beta_u, 0.0)
            gc_u = _dot_table(tri, g_u)
            gcum.append(gc_u)
            gcum_t.append(jnp.concatenate([gc_u, jnp.zeros((LANES - c, LANES), F32)], axis=0).T)
            beta_all.append(beta_u)
            qkv.append(ys[rows[u], :])
            z_all.append(p_ref[rows[u], GDN_Z_COL:GDN_Z_COL + d])

        qn, kn, v, gcol, beta, decay = [], [], [], [], [], []
        for u, h in units:
            qc = qkv[u][:, h * HEAD_DIM:(h + 1) * HEAD_DIM]
            kc = qkv[u][:, d + h * HEAD_DIM:d + (h + 1) * HEAD_DIM]
            vh = qkv[u][:, 2 * d + h * HEAD_DIM:2 * d + (h + 1) * HEAD_DIM]
            qh = qc * lax.rsqrt(jnp.sum(qc * qc, axis=-1, keepdims=True) + EPS) * (HEAD_DIM ** -0.5)
            kh = kc * lax.rsqrt(jnp.sum(kc * kc, axis=-1, keepdims=True) + EPS)
            if t_valid < c:
                qh = jnp.where(live, qh, 0.0)
                kh = jnp.where(live, kh, 0.0)
                vh = jnp.where(live, vh, 0.0)
            qn.append(qh)
            kn.append(kh)
            v.append(vh)
            gc = jnp.broadcast_to(gcum[u][:, h:h + 1], (c, LANES))
            grow = jnp.broadcast_to(gcum_t[u][h:h + 1, 0:c], (c, c))
            gcol.append(gc)
            beta.append(jnp.broadcast_to(beta_all[u][:, N_HEADS + h:N_HEADS + h + 1], (c, LANES)))
            decay.append(jnp.where(incl, jnp.exp(jnp.where(incl, gc[:, 0:c] - grow, 0.0)), 0.0))

        idx = range(len(units))
        kb = [kn[i].astype(BF16) for i in idx]
        both = [_dot_nt(jnp.concatenate([kb[i], qn[i].astype(BF16)], axis=0), kb[i])
                for i in idx]
        low = [jnp.where(strict, beta[i][:, 0:c] * both[i][0:c] * decay[i], 0.0) for i in idx]
        n = [-jnp.where(masks[0], low[i], 0.0) for i in idx]
        for m in masks[1:]:
            cl = [jnp.where(m, low[i], 0.0) for i in idx]
            nb = [n[i].astype(BF16) for i in idx]
            t1 = [cl[i] + _dot(nb[i], cl[i].astype(BF16)) for i in idx]
            n = [n[i] - (t1[i] + _dot(t1[i].astype(BF16), nb[i])) for i in idx]

        eg = [jnp.exp(gcol[i]) for i in idx]
        rhs = [jnp.concatenate([v[i] * beta[i], kn[i] * (beta[i] * eg[i])], axis=1) for i in idx]
        uw = [rhs[i] + _dot(n[i].astype(BF16), rhs[i].astype(BF16)) for i in idx]
        aqk = [(both[i][c:2 * c] * decay[i]).astype(BF16) for i in idx]
        glast = [gcol[i][c - 1:c, :] for i in idx]
        lhs = [jnp.concatenate([uw[i][:, HEAD_DIM:2 * HEAD_DIM].astype(BF16),
                                (qn[i] * eg[i]).astype(BF16)], axis=0) for i in idx]
        kd = [(kn[i] * jnp.exp(glast[i] - gcol[i])).astype(BF16) for i in idx]

        s = [s_ref[h] for h in heads]
        for u in subs:
            ix = [u * N_HEADS + h for h in heads]
            ws = [_dot(lhs[ix[h]], s[h].astype(BF16)) for h in heads]
            vnb = [(uw[ix[h]][:, 0:HEAD_DIM] - ws[h][0:c]).astype(BF16) for h in heads]
            o = [ws[h][c:2 * c] + _dot(aqk[ix[h]], vnb[h]) for h in heads]
            s = [jnp.exp(glast[ix[h]]) * s[h] + _dot_tn(kd[ix[h]], vnb[h]) for h in heads]
            o_all = jnp.concatenate([_rms(o[h], onw) for h in heads], axis=1)
            o_ref[rows[u], :] = o_all * (z_all[u] * _sigmoid(z_all[u]))
        for h in heads:
            s_ref[h] = s[h]
        return carry

    lax.fori_loop(0, n_sub // unroll, body, 0)

    @pl.when(step == pl.num_programs(1) - 1)
    def _():
        sout_ref[...] = s_ref[...]


def _gdn_scan(proj, conv_w, a_log, dt_bias, out_norm, s0, buf, layer, chunk, block, t_valid):
    bsz, t, width = proj.shape
    has_state = s0 is not None
    tri = jnp.asarray(np.tril(np.ones((chunk, chunk), np.float32)), dtype=BF16)

    def const(shape):
        return pl.BlockSpec(shape, lambda b, s: (0, 0))

    def head_row(x):
        return jnp.zeros((1, LANES), F32).at[0, 0:N_HEADS].set(x)

    in_specs = [pl.BlockSpec((None, block, width), lambda b, s: (b, s, 0)),
                const(conv_w.shape), const((1, LANES)), const((1, LANES)), const((1, HEAD_DIM)),
                const(tri.shape)]
    args = [proj, conv_w, head_row(a_log), head_row(dt_bias), out_norm.reshape(1, HEAD_DIM), tri]
    state_spec, state_shape, aliases = _state_io(s0, layer, bsz, len(args))
    if has_state:
        in_specs += [state_spec,
                     pl.BlockSpec((None, GDN_CONV - 1, GDN_CONV_DIM), lambda b, s: (b, 0, 0))]
        args += [s0, buf]
    n_sub = block // chunk
    kern = functools.partial(_gdn_kernel, chunk=chunk, n_sub=n_sub,
                             unroll=min(2 * CHUNK_UNROLL, n_sub), block=block,
                             t_valid=t_valid, has_state=has_state)
    return pl.pallas_call(
        kern,
        grid=(bsz, t // block),
        in_specs=in_specs,
        out_specs=[pl.BlockSpec((None, block, D_MODEL), lambda b, s: (b, s, 0)), state_spec],
        out_shape=[jax.ShapeDtypeStruct((bsz, t, D_MODEL), F32), state_shape],
        input_output_aliases=aliases,
        scratch_shapes=[pltpu.VMEM((N_HEADS, HEAD_DIM, HEAD_DIM), F32),
                        pltpu.VMEM((block + 2 * SUBLANES, GDN_CONV_DIM), F32),
                        pltpu.VMEM((block, GDN_CONV_DIM), F32)],
        compiler_params=pltpu.CompilerParams(
            dimension_semantics=("arbitrary", "arbitrary"), vmem_limit_bytes=VMEM_LIMIT),
        name="gdn_scan",
    )(*args)


def kernel(x_prompt, x_sample, state_hgrn, state_gdn, state_gdn_conv, norm_mix, norm_mlp, norm_final,
           hgrn_w_in, hgrn_lb, hgrn_out_norm, hgrn_w_out, gdn_w_in, gdn_conv_w, gdn_A_log,
           gdn_dt_bias, gdn_out_norm, gdn_w_out, mlp_w_up, mlp_w_down):
    depth = norm_mix.shape[0]
    pb, pt, d = x_prompt.shape
    sb, st, _ = x_sample.shape
    assert pt % PROMPT_BLOCK == 0 and st <= SAMPLE_T_PAD and st >= GDN_CONV - 1

    groups = [
        dict(b=pb, t=pt, tv=PROMPT_CHUNK, chunk=PROMPT_CHUNK, block=PROMPT_BLOCK, x=x_prompt),
        dict(b=sb, t=SAMPLE_T_PAD, tv=st, chunk=SAMPLE_T_PAD, block=SAMPLE_T_PAD,
             x=jnp.pad(x_sample, ((0, 0), (0, SAMPLE_T_PAD - st), (0, 0)))),
    ]
    states = [dict(h=None, g=None, c=None),
              dict(h=state_hgrn, g=state_gdn, c=state_gdn_conv)]
    xs = [g["x"].reshape(g["b"] * g["t"], d) for g in groups]
    new_h, new_g, new_c = ([], []), ([], []), ([], [])
    ys = [None, None]

    for i in range(depth):
        j = i // 2
        last = i == depth - 1
        w_out = (hgrn_w_out if i % 2 == 0 else gdn_w_out)[j].astype(BF16)
        w_up = mlp_w_up[i].astype(BF16)
        w_down = mlp_w_down[i].astype(BF16)
        if i % 2 == 0:
            w_in = hgrn_w_in[j].astype(BF16)
            w_ab = None
        else:
            w_in = gdn_w_in[j][:, 0:GDN_GB_COL].astype(BF16)
            w_ab = gdn_w_in[j][:, GDN_GB_COL:]
            w_ab = jnp.pad(w_ab, ((0, 0), (0, LANES - w_ab.shape[1]))).astype(BF16)
        for gi, (g, s) in enumerate(zip(groups, states)):
            proj = _rms_matmul(xs[gi], norm_mix[i], w_in, w_ab).reshape(g["b"], g["t"], -1)
            if i % 2 == 0:
                o, s_new = _hgrn_scan(proj, hgrn_lb, hgrn_out_norm[j], s["h"],
                                      j, g["chunk"], g["block"], g["tv"])
                if s["h"] is None:
                    new_h[gi].append(s_new)
                else:
                    s["h"] = s_new
            else:
                o, s_new = _gdn_scan(proj, gdn_conv_w[j], gdn_A_log[j], gdn_dt_bias[j],
                                     gdn_out_norm[j], s["g"],
                                     None if s["c"] is None else s["c"][j],
                                     j, g["chunk"], g["block"], g["tv"])
                if s["g"] is None:
                    new_g[gi].append(s_new)
                else:
                    s["g"] = s_new
                tv = g["tv"] if gi == 1 else g["t"]
                new_c[gi].append(proj[:, tv - (GDN_CONV - 1):tv, 0:GDN_CONV_DIM])
            res = _out_mlp(o.reshape(-1, d), xs[gi], w_out, norm_mlp[i], w_up, w_down,
                           norm_final, last)
            xs[gi] = res[0]
            if last:
                ys[gi] = res[1]

    y_prompt = ys[0].reshape(pb, pt, d)
    y_sample = ys[1].reshape(sb, SAMPLE_T_PAD, d)[:, 0:st]
    return (y_prompt, y_sample, jnp.stack(new_h[0]), states[1]["h"],
            jnp.stack(new_g[0]), states[1]["g"], jnp.stack(new_c[0]), jnp.stack(new_c[1]))
```

```python
import functools

import jax
import jax.numpy as jnp
import numpy as np
from jax import lax
from jax.experimental import pallas as pl
from jax.experimental.pallas import tpu as pltpu

F32 = jnp.float32
BF16 = jnp.bfloat16

D_MODEL = 1024
N_HEADS = 8
HEAD_DIM = 128
D_FF = 4 * D_MODEL
FF_CHUNK = 1024
GDN_CONV = 4
GDN_CONV_DIM = 3 * D_MODEL
GDN_Z_COL = GDN_CONV_DIM
GDN_GB_COL = GDN_CONV_DIM + D_MODEL
GDN_IN_PAD = GDN_GB_COL + HEAD_DIM
EPS = 1e-6
LB_FLOOR = 1e-30
LANES = 128
SUBLANES = 8
ROW_TILE = 512
PROMPT_CHUNK = 64
PROMPT_BLOCK = 256
CHUNK_UNROLL = 2
SAMPLE_T_PAD = 8
VMEM_LIMIT = 56 * 1024 * 1024


def _dot(a, b):
    return jnp.dot(a, b, preferred_element_type=F32)


def _dot_nt(a, b):
    return lax.dot_general(a, b, (((1,), (1,)), ((), ())), preferred_element_type=F32)


def _dot_tn(a, b):
    return lax.dot_general(a, b, (((0,), (0,)), ((), ())), preferred_element_type=F32)


def _dot_table(table, x):
    hi = x.astype(BF16)
    r = x - hi.astype(F32)
    mid = r.astype(BF16)
    lo = (r - mid.astype(F32)).astype(BF16)
    return _dot(table, hi) + _dot(table, mid) + _dot(table, lo)


def _sigmoid(x):
    return 1.0 / (1.0 + jnp.exp(-x))


def _softplus(x):
    return jnp.maximum(x, 0.0) + jnp.log1p(jnp.exp(-jnp.abs(x)))


def _rms(x, w):
    return x * lax.rsqrt(jnp.mean(x * x, axis=-1, keepdims=True) + EPS) * w


def _level_masks(c):
    tt = lax.broadcasted_iota(jnp.int32, (c, c), 0)
    ss = lax.broadcasted_iota(jnp.int32, (c, c), 1)
    levels = int(np.log2(c))
    assert 2 ** levels == c
    return [(((tt >> l) & 1) == 1) & (((ss >> l) & 1) == 0) & ((tt >> (l + 1)) == (ss >> (l + 1)))
            for l in range(levels)], tt, ss


def _state_io(s0, layer, bsz, s0_arg):
    if s0 is None:
        spec = pl.BlockSpec((None, N_HEADS, HEAD_DIM, HEAD_DIM), lambda b, s: (b, 0, 0, 0))
        return spec, jax.ShapeDtypeStruct((bsz, N_HEADS, HEAD_DIM, HEAD_DIM), F32), {}
    spec = pl.BlockSpec((None, None, N_HEADS, HEAD_DIM, HEAD_DIM),
                        lambda b, s: (layer, b, 0, 0, 0))
    return spec, jax.ShapeDtypeStruct(s0.shape, F32), {s0_arg: 1}


def _rms_matmul_kernel(x_ref, nw_ref, w_ref, *rest):
    h = _rms(x_ref[...], nw_ref[...]).astype(BF16)
    if len(rest) == 1:
        rest[0][...] = _dot(h, w_ref[...])
    else:
        w2_ref, o_ref = rest
        n = w_ref.shape[1]
        o_ref[:, 0:n] = _dot(h, w_ref[...])
        o_ref[:, n:] = _dot(h, w2_ref[...])


def _rms_matmul(x, nw, w, w2=None):
    m, d = x.shape
    n = w.shape[1] + (0 if w2 is None else w2.shape[1])
    tm = min(ROW_TILE, m)

    def whole(shape):
        return pl.BlockSpec(shape, lambda i: (0, 0), pipeline_mode=pl.Buffered(1))

    in_specs = [pl.BlockSpec((tm, d), lambda i: (i, 0)), pl.BlockSpec((1, d), lambda i: (0, 0)),
                whole(w.shape)]
    args = [x, nw.reshape(1, d), w]
    if w2 is not None:
        in_specs.append(whole(w2.shape))
        args.append(w2)
    return pl.pallas_call(
        _rms_matmul_kernel,
        grid=(m // tm,),
        in_specs=in_specs,
        out_specs=pl.BlockSpec((tm, n), lambda i: (i, 0)),
        out_shape=jax.ShapeDtypeStruct((m, n), F32),
        compiler_params=pltpu.CompilerParams(
            dimension_semantics=("arbitrary",), vmem_limit_bytes=VMEM_LIMIT),
        name="rms_matmul",
    )(*args)


def _out_mlp_kernel(o_ref, x_ref, wo_ref, nw_ref, wu_ref, wd_ref, nf_ref, xo_ref, y_ref=None):
    x1 = x_ref[...] + _dot(o_ref[...].astype(BF16), wo_ref[...])
    h = _rms(x1, nw_ref[...]).astype(BF16)
    acc = x1
    for c in range(D_FF // FF_CHUNK):
        u = _dot(h, wu_ref[:, c * FF_CHUNK:(c + 1) * FF_CHUNK])
        a = jnp.square(jnp.maximum(u, 0.0)).astype(BF16)
        acc = acc + _dot(a, wd_ref[c * FF_CHUNK:(c + 1) * FF_CHUNK, :])
    xo_ref[...] = acc
    if y_ref is not None:
        y_ref[...] = _rms(acc, nf_ref[...])


def _out_mlp(o, x, w_out, nw, w_up, w_down, nf, final):
    m, d = x.shape
    tm = min(ROW_TILE, m)
    row = pl.BlockSpec((tm, d), lambda i: (i, 0))
    vec = pl.BlockSpec((1, d), lambda i: (0, 0))

    def whole(shape):
        return pl.BlockSpec(shape, lambda i: (0, 0), pipeline_mode=pl.Buffered(1))

    out_shape = [jax.ShapeDtypeStruct((m, d), F32)]
    out_specs = [row]
    if final:
        out_shape.append(jax.ShapeDtypeStruct((m, d), F32))
        out_specs.append(row)
    return pl.pallas_call(
        _out_mlp_kernel,
        grid=(m // tm,),
        in_specs=[row, row, whole(w_out.shape), vec, whole(w_up.shape), whole(w_down.shape), vec],
        out_specs=out_specs,
        out_shape=out_shape,
        compiler_params=pltpu.CompilerParams(
            dimension_semantics=("arbitrary",), vmem_limit_bytes=VMEM_LIMIT),
        name="out_mlp",
    )(o, x, w_out, nw.reshape(1, d), w_up, w_down, nf.reshape(1, d))


def _hgrn_tables(c):
    levels = int(np.log2(c))
    t = np.arange(c)[:, None]
    r = np.arange(c)[None, :]
    summing = [(r <= t), (r > t)]
    for l in range(levels):
        h = 2 ** l
        start = (t // (2 * h)) * (2 * h)
        upper = (t - start) >= h
        ref = start + h - 1
        summing.append(np.where(upper, (r > ref) & (r <= t), (r > t) & (r <= ref)))
    return np.concatenate(summing, 0).astype(np.float32)


def _hgrn_kernel(*refs, layer, chunk, n_sub, unroll, t_valid, has_state):
    if has_state:
        p_ref, lb_ref, on_ref, sum_ref, s0_ref, o_ref, sout_ref, st_ref = refs
    else:
        p_ref, lb_ref, on_ref, sum_ref, o_ref, sout_ref, st_ref = refs
    step = pl.program_id(1)
    c = chunk
    d = D_MODEL

    @pl.when(step == 0)
    def _():
        for h in range(N_HEADS):
            if has_state:
                st_ref[h] = s0_ref[h].T
            else:
                st_ref[h] = jnp.zeros((HEAD_DIM, HEAD_DIM), F32)

    lbraw = lb_ref[...]
    e = jnp.exp(lbraw - jnp.max(lbraw, axis=0, keepdims=True))
    sm = e / jnp.sum(e, axis=0, keepdims=True)
    lb = sm[0:1, :]
    for i in range(1, layer + 1):
        lb = lb + sm[i:i + 1, :]
    lb = lb - sm[0:1, :]
    lb_floor = jnp.maximum(lb, LB_FLOOR)
    one_m_lb = 1.0 - lb

    masks, tt, ss = _level_masks(c)
    diag = tt == ss
    row = lax.broadcasted_iota(jnp.int32, (c, HEAD_DIM), 0)
    uppers = [((row >> l) & 1) == 1 for l in range(len(masks))]
    onw = on_ref[...]
    table = sum_ref[...]

    def body(j, carry):
        subs = range(unroll)
        heads = range(N_HEADS)
        units = [(u, h) for u in subs for h in heads]
        idx = range(len(units))
        cs = [slice(h * HEAD_DIM, (h + 1) * HEAD_DIM) for h in heads]
        rows = [pl.ds(pl.multiple_of((j * unroll + u) * c, c), c) for u in subs]
        q_all, k_all, v_all, gate_all, ex_all = [], [], [], [], []
        for u in subs:
            qr = p_ref[rows[u], 0:d]
            fz = p_ref[rows[u], d:2 * d]
            v_u = p_ref[rows[u], 2 * d:3 * d]
            gr = p_ref[rows[u], 3 * d:4 * d]
            q_u = qr * _sigmoid(qr)
            gate_all.append(gr * _sigmoid(gr))
            ez = jnp.exp(-jnp.abs(fz))
            rz = 1.0 / (1.0 + ez)
            pos = fz >= 0.0
            lf_u = jnp.log(lb_floor + one_m_lb * jnp.where(pos, rz, ez * rz))
            k_u = one_m_lb * jnp.where(pos, ez * rz, rz)
            if t_valid < c:
                live = lax.broadcasted_iota(jnp.int32, (c, d), 0) < t_valid
                q_u = jnp.where(live, q_u, 0.0)
                k_u = jnp.where(live, k_u, 0.0)
                v_u = jnp.where(live, v_u, 0.0)
                lf_u = jnp.where(live, lf_u, 0.0)
            q_all.append(q_u)
            k_all.append(k_u)
            v_all.append(v_u)
            ex_all.append(jnp.exp(_dot_table(table, lf_u)))

        q = [q_all[u][:, cs[h]] for u, h in units]
        k = [k_all[u][:, cs[h]] for u, h in units]
        vb = [v_all[u][:, cs[h]].astype(BF16) for u, h in units]

        def ex(i, lo, hi):
            u, h = units[i]
            return ex_all[u][lo:hi, cs[h]]

        a = [jnp.where(diag, _dot_nt(q[i].astype(BF16), k[i].astype(BF16)), 0.0) for i in idx]
        for l, (m, upper) in enumerate(zip(masks, uppers)):
            x = [(jnp.where(upper, q[i], k[i]) * ex(i, (2 + l) * c, (3 + l) * c)).astype(BF16)
                 for i in idx]
            a = [a[i] + jnp.where(m, _dot_nt(x[i], x[i]), 0.0) for i in idx]
        o_intra = [_dot(a[i].astype(BF16), vb[i]) for i in idx]
        qd = [(q[i] * ex(i, 0, c)).astype(BF16) for i in idx]
        kh = [(k[i] * ex(i, c, 2 * c)).astype(BF16) for i in idx]
        kv = [_dot_tn(vb[i], kh[i]) for i in idx]

        st = [st_ref[h] for h in heads]
        for u in subs:
            ix = [u * N_HEADS + h for h in heads]
            o = [o_intra[ix[h]] + _dot_nt(qd[ix[h]], st[h].astype(BF16)) for h in heads]
            st = [st[h] * ex(ix[h], c - 1, c) + kv[ix[h]] for h in heads]
            o_all = jnp.concatenate([_rms(o[h], onw) for h in heads], axis=1)
            o_ref[rows[u], :] = o_all * gate_all[u]
        for h in heads:
            st_ref[h] = st[h]
        return carry

    lax.fori_loop(0, n_sub // unroll, body, 0)

    @pl.when(step == pl.num_programs(1) - 1)
    def _():
        for h in range(N_HEADS):
            sout_ref[h] = st_ref[h].T


def _hgrn_scan(proj, lb_raw, out_norm, s0, layer, chunk, block, t_valid):
    bsz, t, width = proj.shape
    table = jnp.asarray(_hgrn_tables(chunk), dtype=BF16)
    has_state = s0 is not None

    def const(shape):
        return pl.BlockSpec(shape, lambda b, s: (0, 0))

    in_specs = [pl.BlockSpec((None, block, width), lambda b, s: (b, s, 0)),
                const(lb_raw.shape), const((1, HEAD_DIM)), const(table.shape)]
    args = [proj, lb_raw, out_norm.reshape(1, HEAD_DIM), table]
    state_spec, state_shape, aliases = _state_io(s0, layer, bsz, len(args))
    if has_state:
        in_specs.append(state_spec)
        args.append(s0)
    n_sub = block // chunk
    kern = functools.partial(_hgrn_kernel, layer=layer, chunk=chunk, n_sub=n_sub,
                             unroll=min(2 * CHUNK_UNROLL, n_sub), t_valid=t_valid,
                             has_state=has_state)
    return pl.pallas_call(
        kern,
        grid=(bsz, t // block),
        in_specs=in_specs,
        out_specs=[pl.BlockSpec((None, block, D_MODEL), lambda b, s: (b, s, 0)), state_spec],
        out_shape=[jax.ShapeDtypeStruct((bsz, t, D_MODEL), F32), state_shape],
        input_output_aliases=aliases,
        scratch_shapes=[pltpu.VMEM((N_HEADS, HEAD_DIM, HEAD_DIM), F32)],
        compiler_params=pltpu.CompilerParams(
            dimension_semantics=("arbitrary", "arbitrary"), vmem_limit_bytes=VMEM_LIMIT),
        name="hgrn_scan",
    )(*args)


def _gdn_kernel(*refs, chunk, n_sub, unroll, block, t_valid, has_state):
    if has_state:
        (p_ref, cw_ref, al_ref, dt_ref, on_ref, tri_ref, s0_ref, buf_ref,
         o_ref, sout_ref, s_ref, xs, ys) = refs
    else:
        (p_ref, cw_ref, al_ref, dt_ref, on_ref, tri_ref,
         o_ref, sout_ref, s_ref, xs, ys) = refs
    step = pl.program_id(1)
    c = chunk
    d = D_MODEL
    tail = SUBLANES
    taps = GDN_CONV - 1

    @pl.when(step == 0)
    def _():
        if has_state:
            s_ref[...] = s0_ref[...]
            xs[tail - taps:tail, :] = buf_ref[...]
        else:
            s_ref[...] = jnp.zeros_like(s_ref)
            xs[0:tail, :] = jnp.zeros((tail, GDN_CONV_DIM), F32)

    xs[tail:tail + block, :] = p_ref[:, 0:GDN_CONV_DIM]
    y = xs[pl.ds(tail - taps, block), :] * cw_ref[0:1, :]
    for j in range(1, GDN_CONV):
        y = y + xs[pl.ds(tail - taps + j, block), :] * cw_ref[j:j + 1, :]
    ys[...] = y * _sigmoid(y)
    xs[0:tail, :] = xs[block:block + tail, :]

    masks, tt, ss = _level_masks(c)
    incl = ss <= tt
    strict = ss < tt
    onw = on_ref[...]
    neg_a = -jnp.exp(al_ref[...])
    dtb = dt_ref[...]
    tri = tri_ref[...]

    def body(j, carry):
        subs = range(unroll)
        heads = range(N_HEADS)
        units = [(u, h) for u in subs for h in heads]
        rows = [pl.ds(pl.multiple_of((j * unroll + u) * c, c), c) for u in subs]
        gcum, gcum_t, beta_all, qkv, z_all = [], [], [], [], []
        for u in subs:
            gb = p_ref[rows[u], GDN_GB_COL:GDN_GB_COL + LANES]
            g_u = neg_a * _softplus(gb + dtb)
            beta_u = _sigmoid(gb)
            if t_valid < c:
                live = lax.broadcasted_iota(jnp.int32, (c, LANES), 0) < t_valid
                g_u = jnp.where(live, g_u, 0.0)
                beta_u = jnp.where(live, beta_u, 0.0)
            gc_u = _dot_table(tri, g_u)
            gcum.append(gc_u)
            gcum_t.append(jnp.concatenate([gc_u, jnp.zeros((LANES - c, LANES), F32)], axis=0).T)
            beta_all.append(beta_u)
            qkv.append(ys[rows[u], :])
            z_all.append(p_ref[rows[u], GDN_Z_COL:GDN_Z_COL + d])

        qn, kn, v, gcol, beta, decay = [], [], [], [], [], []
        for u, h in units:
            qc = qkv[u][:, h * HEAD_DIM:(h + 1) * HEAD_DIM]
            kc = qkv[u][:, d + h * HEAD_DIM:d + (h + 1) * HEAD_DIM]
            vh = qkv[u][:, 2 * d + h * HEAD_DIM:2 * d + (h + 1) * HEAD_DIM]
            qh = qc * lax.rsqrt(jnp.sum(qc * qc, axis=-1, keepdims=True) + EPS) * (HEAD_DIM ** -0.5)
            kh = kc * lax.rsqrt(jnp.sum(kc * kc, axis=-1, keepdims=True) + EPS)
            if t_valid < c:
                qh = jnp.where(live, qh, 0.0)
                kh = jnp.where(live, kh, 0.0)
                vh = jnp.where(live, vh, 0.0)
            qn.append(qh)
            kn.append(kh)
            v.append(vh)
            gc = jnp.broadcast_to(gcum[u][:, h:h + 1], (c, LANES))
            grow = jnp.broadcast_to(gcum_t[u][h:h + 1, 0:c], (c, c))
            gcol.append(gc)
            beta.append(jnp.broadcast_to(beta_all[u][:, N_HEADS + h:N_HEADS + h + 1], (c, LANES)))
            decay.append(jnp.where(incl, jnp.exp(jnp.where(incl, gc[:, 0:c] - grow, 0.0)), 0.0))

        idx = range(len(units))
        kb = [kn[i].astype(BF16) for i in idx]
        both = [_dot_nt(jnp.concatenate([kb[i], qn[i].astype(BF16)], axis=0), kb[i])
                for i in idx]
        low = [jnp.where(strict, beta[i][:, 0:c] * both[i][0:c] * decay[i], 0.0) for i in idx]
        n = [-jnp.where(masks[0], low[i], 0.0) for i in idx]
        for m in masks[1:]:
            cl = [jnp.where(m, low[i], 0.0) for i in idx]
            nb = [n[i].astype(BF16) for i in idx]
            t1 = [cl[i] + _dot(nb[i], cl[i].astype(BF16)) for i in idx]
            n = [n[i] - (t1[i] + _dot(t1[i].astype(BF16), nb[i])) for i in idx]

        eg = [jnp.exp(gcol[i]) for i in idx]
        rhs = [jnp.concatenate([v[i] * beta[i], kn[i] * (beta[i] * eg[i])], axis=1) for i in idx]
        uw = [rhs[i] + _dot(n[i].astype(BF16), rhs[i].astype(BF16)) for i in idx]
        aqk = [(both[i][c:2 * c] * decay[i]).astype(BF16) for i in idx]
        glast = [gcol[i][c - 1:c, :] for i in idx]
        lhs = [jnp.concatenate([uw[i][:, HEAD_DIM:2 * HEAD_DIM].astype(BF16),
                                (qn[i] * eg[i]).astype(BF16)], axis=0) for i in idx]
        kd = [(kn[i] * jnp.exp(glast[i] - gcol[i])).astype(BF16) for i in idx]

        s = [s_ref[h] for h in heads]
        for u in subs:
            ix = [u * N_HEADS + h for h in heads]
            ws = [_dot(lhs[ix[h]], s[h].astype(BF16)) for h in heads]
            vnb = [(uw[ix[h]][:, 0:HEAD_DIM] - ws[h][0:c]).astype(BF16) for h in heads]
            o = [ws[h][c:2 * c] + _dot(aqk[ix[h]], vnb[h]) for h in heads]
            s = [jnp.exp(glast[ix[h]]) * s[h] + _dot_tn(kd[ix[h]], vnb[h]) for h in heads]
            o_all = jnp.concatenate([_rms(o[h], onw) for h in heads], axis=1)
            o_ref[rows[u], :] = o_all * (z_all[u] * _sigmoid(z_all[u]))
        for h in heads:
            s_ref[h] = s[h]
        return carry

    lax.fori_loop(0, n_sub // unroll, body, 0)

    @pl.when(step == pl.num_programs(1) - 1)
    def _():
        sout_ref[...] = s_ref[...]


def _gdn_scan(proj, conv_w, a_log, dt_bias, out_norm, s0, buf, layer, chunk, block, t_valid):
    bsz, t, width = proj.shape
    has_state = s0 is not None
    tri = jnp.asarray(np.tril(np.ones((chunk, chunk), np.float32)), dtype=BF16)

    def const(shape):
        return pl.BlockSpec(shape, lambda b, s: (0, 0))

    def head_row(x):
        return jnp.zeros((1, LANES), F32).at[0, 0:N_HEADS].set(x)

    in_specs = [pl.BlockSpec((None, block, width), lambda b, s: (b, s, 0)),
                const(conv_w.shape), const((1, LANES)), const((1, LANES)), const((1, HEAD_DIM)),
                const(tri.shape)]
    args = [proj, conv_w, head_row(a_log), head_row(dt_bias), out_norm.reshape(1, HEAD_DIM), tri]
    state_spec, state_shape, aliases = _state_io(s0, layer, bsz, len(args))
    if has_state:
        in_specs += [state_spec,
                     pl.BlockSpec((None, GDN_CONV - 1, GDN_CONV_DIM), lambda b, s: (b, 0, 0))]
        args += [s0, buf]
    n_sub = block // chunk
    kern = functools.partial(_gdn_kernel, chunk=chunk, n_sub=n_sub,
                             unroll=min(2 * CHUNK_UNROLL, n_sub), block=block,
                             t_valid=t_valid, has_state=has_state)
    return pl.pallas_call(
        kern,
        grid=(bsz, t // block),
        in_specs=in_specs,
        out_specs=[pl.BlockSpec((None, block, D_MODEL), lambda b, s: (b, s, 0)), state_spec],
        out_shape=[jax.ShapeDtypeStruct((bsz, t, D_MODEL), F32), state_shape],
        input_output_aliases=aliases,
        scratch_shapes=[pltpu.VMEM((N_HEADS, HEAD_DIM, HEAD_DIM), F32),
                        pltpu.VMEM((block + 2 * SUBLANES, GDN_CONV_DIM), F32),
                        pltpu.VMEM((block, GDN_CONV_DIM), F32)],
        compiler_params=pltpu.CompilerParams(
            dimension_semantics=("arbitrary", "arbitrary"), vmem_limit_bytes=VMEM_LIMIT),
        name="gdn_scan",
    )(*args)


def kernel(x_prompt, x_sample, state_hgrn, state_gdn, state_gdn_conv, norm_mix, norm_mlp, norm_final,
           hgrn_w_in, hgrn_lb, hgrn_out_norm, hgrn_w_out, gdn_w_in, gdn_conv_w, gdn_A_log,
           gdn_dt_bias, gdn_out_norm, gdn_w_out, mlp_w_up, mlp_w_down):
    depth = norm_mix.shape[0]
    pb, pt, d = x_prompt.shape
    sb, st, _ = x_sample.shape
    assert pt % PROMPT_BLOCK == 0 and st <= SAMPLE_T_PAD and st >= GDN_CONV - 1

    groups = [
        dict(b=pb, t=pt, tv=PROMPT_CHUNK, chunk=PROMPT_CHUNK, block=PROMPT_BLOCK, x=x_prompt),
        dict(b=sb, t=SAMPLE_T_PAD, tv=st, chunk=SAMPLE_T_PAD, block=SAMPLE_T_PAD,
             x=jnp.pad(x_sample, ((0, 0), (0, SAMPLE_T_PAD - st), (0, 0)))),
    ]
    states = [dict(h=None, g=None, c=None),
              dict(h=state_hgrn, g=state_gdn, c=state_gdn_conv)]
    xs = [g["x"].reshape(g["b"] * g["t"], d) for g in groups]
    new_h, new_g, new_c = ([], []), ([], []), ([], [])
    ys = [None, None]

    for i in range(depth):
        j = i // 2
        last = i == depth - 1
        w_out = (hgrn_w_out if i % 2 == 0 else gdn_w_out)[j].astype(BF16)
        w_up = mlp_w_up[i].astype(BF16)
        w_down = mlp_w_down[i].astype(BF16)
        if i % 2 == 0:
            w_in = hgrn_w_in[j].astype(BF16)
            w_ab = None
        else:
            w_in = gdn_w_in[j][:, 0:GDN_GB_COL].astype(BF16)
            w_ab = gdn_w_in[j][:, GDN_GB_COL:]
            w_ab = jnp.pad(w_ab, ((0, 0), (0, LANES - w_ab.shape[1]))).astype(BF16)
        for gi, (g, s) in enumerate(zip(groups, states)):
            proj = _rms_matmul(xs[gi], norm_mix[i], w_in, w_ab).reshape(g["b"], g["t"], -1)
            if i % 2 == 0:
                o, s_new = _hgrn_scan(proj, hgrn_lb, hgrn_out_norm[j], s["h"],
                                      j, g["chunk"], g["block"], g["tv"])
                if s["h"] is None:
                    new_h[gi].append(s_new)
                else:
                    s["h"] = s_new
            else:
                o, s_new = _gdn_scan(proj, gdn_conv_w[j], gdn_A_log[j], gdn_dt_bias[j],
                                     gdn_out_norm[j], s["g"],
                                     None if s["c"] is None else s["c"][j],
                                     j, g["chunk"], g["block"], g["tv"])
                if s["g"] is None:
                    new_g[gi].append(s_new)
                else:
                    s["g"] = s_new
                tv = g["tv"] if gi == 1 else g["t"]
                new_c[gi].append(proj[:, tv - (GDN_CONV - 1):tv, 0:GDN_CONV_DIM])
            res = _out_mlp(o.reshape(-1, d), xs[gi], w_out, norm_mlp[i], w_up, w_down,
                           norm_final, last)
            xs[gi] = res[0]
            if last:
                ys[gi] = res[1]

    y_prompt = ys[0].reshape(pb, pt, d)
    y_sample = ys[1].reshape(sb, SAMPLE_T_PAD, d)[:, 0:st]
    return (y_prompt, y_sample, jnp.stack(new_h[0]), states[1]["h"],
            jnp.stack(new_g[0]), states[1]["g"], jnp.stack(new_c[0]), jnp.stack(new_c[1]))
```
